```python
import jax, jax.numpy as jnp
from jax import lax
import numpy as np

D_MODEL = 2048
BATCH = 4
SEQ = 2048
DEPTH = 4
DEC_BATCH = 128
DEC_SEQ = 1
PAST_LEN = 16384
PAGE_SIZE = 128

N_META = 16
N_MIXERS = 2
N_POOL_LAYERS = (DEPTH + 1) // 2
N_RET_LAYERS = DEPTH // 2
POOL_WINDOWS = (2, 4, 8, 16)
POOL_GROUPS = len(POOL_WINDOWS)
POOL_GROUP_DIM = D_MODEL // POOL_GROUPS
POOL_BUF = max(POOL_WINDOWS) - 1
RET_HEADS = 8
RET_DK = D_MODEL // RET_HEADS
RET_DV = 2 * RET_DK
RET_VDIM = RET_HEADS * RET_DV
RET_IN = 2 * D_MODEL + 2 * RET_VDIM
RET_CHUNK = 128
ROPE_BASE = 10000.0
D_FF = 4 * D_MODEL
EPS = 1e-6

kernel_name = "pool_retention_hybrid_step"


def rmsnorm(x, g):
    xf = x.astype(jnp.float32)
    y = xf * lax.rsqrt(jnp.mean(xf * xf, axis=-1, keepdims=True) + EPS)
    return y.astype(x.dtype) * g


def head_norm(o):
    of = o.astype(jnp.float32)
    mu = jnp.mean(of, axis=-1, keepdims=True)
    var = jnp.mean(jnp.square(of - mu), axis=-1, keepdims=True)
    return ((of - mu) * lax.rsqrt(var + EPS)).astype(o.dtype)


def rope(x, pos):
    half = x.shape[-1] // 2
    inv = ROPE_BASE ** (-jnp.arange(half, dtype=jnp.float32) / half)
    ang = pos.astype(jnp.float32)[:, None] * inv[None, :]
    cos = jnp.cos(ang)[None, :, None, :].astype(x.dtype)
    sin = jnp.sin(ang)[None, :, None, :].astype(x.dtype)
    x1, x2 = x[..., :half], x[..., half:]
    return jnp.concatenate([x1 * cos - x2 * sin, x1 * sin + x2 * cos], axis=-1)


def retention_chunk(q, k, v, s, log_gamma):
    c = q.shape[1]
    idx = jnp.arange(c, dtype=jnp.float32)
    diff = idx[:, None] - idx[None, :]
    decay = jnp.where(diff[None] >= 0,
                      jnp.exp(jnp.maximum(diff, 0.0)[None] * log_gamma[:, None, None]),
                      0.0).astype(q.dtype)
    scores = jnp.einsum('bihd,bjhd->bhij', q, k) * decay[None]
    inner = jnp.einsum('bhij,bjhv->bihv', scores, v)
    q_dec = jnp.exp((idx + 1.0)[:, None] * log_gamma[None, :]).astype(q.dtype)
    cross = jnp.einsum('bihd,bhdv->bihv', q, s).astype(q.dtype) * q_dec[None, :, :, None]
    k_dec = jnp.exp((c - 1.0 - idx)[:, None] * log_gamma[None, :]).astype(k.dtype)
    s_dec = jnp.exp(c * log_gamma).astype(s.dtype)
    s_new = s * s_dec[None, :, None, None] + jnp.einsum(
        'bjhd,bjhv->bhdv', k * k_dec[None, :, :, None], v).astype(s.dtype)
    return inner + cross, s_new


def pool_layer(x, prev, pos0, norm_g, w_groups, scale):
    b, t, _ = x.shape
    u = rmsnorm(x, norm_g)
    ext = jnp.concatenate([prev.astype(u.dtype), u], axis=1)
    extf = ext.astype(jnp.float32)
    cs = jnp.concatenate([jnp.zeros((b, 1, D_MODEL), jnp.float32),
                          jnp.cumsum(extf, axis=1)], axis=1)
    end = cs[:, POOL_BUF + 1:]
    pos = pos0 + jnp.arange(t)
    outs = []
    for gi, w in enumerate(POOL_WINDOWS):
        sl = slice(gi * POOL_GROUP_DIM, (gi + 1) * POOL_GROUP_DIM)
        start = cs[:, POOL_BUF + 1 - w:POOL_BUF + 1 - w + t, sl]
        cnt = jnp.minimum(w, pos + 1).astype(jnp.float32)[None, :, None]
        outs.append((end[..., sl] - start) / cnt - extf[:, POOL_BUF:, sl])
    d = jnp.stack(outs, axis=2).astype(x.dtype)
    y = jnp.einsum('btgc,gcd->btgd', d, w_groups).reshape(b, t, D_MODEL) * scale
    return x + y, ext[:, -POOL_BUF:]


def retention_layer(x, pos, s, norm_g, w_in, w_out, log_gamma, n_lead):
    b, t, _ = x.shape
    u = rmsnorm(x, norm_g)
    qkvg = u @ w_in
    q = rope(qkvg[..., :D_MODEL].reshape(b, t, RET_HEADS, RET_DK), pos)
    k = rope(qkvg[..., D_MODEL:2 * D_MODEL].reshape(b, t, RET_HEADS, RET_DK), pos) * (RET_DK ** -0.5)
    v = qkvg[..., 2 * D_MODEL:2 * D_MODEL + RET_VDIM].reshape(b, t, RET_HEADS, RET_DV)
    g = qkvg[..., 2 * D_MODEL + RET_VDIM:]
    if n_lead > 0:
        o_lead, s = retention_chunk(q[:, :n_lead], k[:, :n_lead], v[:, :n_lead], s, log_gamma)
        n_chunks = (t - n_lead) // RET_CHUNK
        def to_chunks(a):
            return jnp.moveaxis(a[:, n_lead:].reshape(b, n_chunks, RET_CHUNK, RET_HEADS, a.shape[-1]), 1, 0)

        def step(carry, inp):
            qc, kc, vc = inp
            o_c, carry = retention_chunk(qc, kc, vc, carry, log_gamma)
            return carry, o_c
        s, o_r = lax.scan(step, s, (to_chunks(q), to_chunks(k), to_chunks(v)))
        o_r = jnp.moveaxis(o_r, 0, 1).reshape(b, t - n_lead, RET_HEADS, RET_DV)
        o = jnp.concatenate([o_lead, o_r], axis=1)
    else:
        o, s = retention_chunk(q, k, v, s, log_gamma)
    y = head_norm(o).reshape(b, t, RET_VDIM) * jax.nn.silu(g)
    return x + y @ w_out, s


def mlp_layer(x, norm_g, w_up, w_down):
    h = jnp.square(jax.nn.relu(rmsnorm(x, norm_g) @ w_up))
    return x + h @ w_down


def setup_inputs(seed: int = 0) -> dict:
    key = jax.random.key(seed)
    ks = jax.random.split(key, 16)
    f32 = jnp.float32
    nrm = lambda k, shp, sc: jax.random.normal(k, shp, f32) * sc
    return {
        "x_prompt": nrm(ks[0], (BATCH, SEQ, D_MODEL), 1.0),
        "x_sample": nrm(ks[1], (DEC_BATCH, DEC_SEQ, D_MODEL), 1.0),
        "state_pool": nrm(ks[2], (N_POOL_LAYERS, DEC_BATCH, POOL_BUF, D_MODEL), 1.0),
        "state_ret": nrm(ks[3], (N_RET_LAYERS, DEC_BATCH, RET_HEADS, RET_DK, RET_DV), 0.5),
        "meta_tokens": nrm(ks[4], (N_META, D_MODEL), 1.0),
        "pool_norm": 1.0 + nrm(ks[5], (N_POOL_LAYERS, D_MODEL), 0.02),
        "pool_w": nrm(ks[6], (N_POOL_LAYERS, POOL_GROUPS, POOL_GROUP_DIM, POOL_GROUP_DIM), POOL_GROUP_DIM ** -0.5),
        "pool_scale": 1.0 + nrm(ks[7], (N_POOL_LAYERS, D_MODEL), 0.02),
        "ret_norm": 1.0 + nrm(ks[8], (N_RET_LAYERS, D_MODEL), 0.02),
        "ret_w_in": nrm(ks[9], (N_RET_LAYERS, D_MODEL, RET_IN), D_MODEL ** -0.5),
        "ret_w_out": nrm(ks[10], (N_RET_LAYERS, RET_VDIM, D_MODEL), RET_VDIM ** -0.5),
        "mlp_norm": 1.0 + nrm(ks[11], (DEPTH, D_MODEL), 0.02),
        "mlp_w_up": nrm(ks[12], (DEPTH, D_MODEL, D_FF), D_MODEL ** -0.5),
        "mlp_w_down": nrm(ks[13], (DEPTH, D_FF, D_MODEL), D_FF ** -0.5),
        "final_norm": 1.0 + nrm(ks[14], (D_MODEL,), 0.02),
    }


def reference(x_prompt, x_sample, state_pool, state_ret, meta_tokens, pool_norm, pool_w, pool_scale,
              ret_norm, ret_w_in, ret_w_out, mlp_norm, mlp_w_up, mlp_w_down, final_norm):
    b = x_prompt.shape[0]
    db, ds = x_sample.shape[0], x_sample.shape[1]
    xp = jnp.concatenate([jnp.broadcast_to(meta_tokens[None].astype(x_prompt.dtype), (b, N_META, D_MODEL)),
                          x_prompt], axis=1)
    xs = x_sample
    pos_p = jnp.arange(xp.shape[1])
    pos_s = PAST_LEN + jnp.arange(ds)
    log_gamma = jnp.log1p(-jnp.exp2(-5.0 - jnp.arange(RET_HEADS, dtype=jnp.float32)))
    pool_p, pool_s, ret_p, ret_s = [], [], [], []
    for i in range(DEPTH):
        j = i // N_MIXERS
        if i % N_MIXERS == 0:
            xp, bp = pool_layer(xp, jnp.zeros((b, POOL_BUF, D_MODEL), xp.dtype), 0,
                                pool_norm[j], pool_w[j], pool_scale[j])
            xs, bs = pool_layer(xs, state_pool[j], PAST_LEN, pool_norm[j], pool_w[j], pool_scale[j])
            pool_p.append(bp)
            pool_s.append(bs)
        else:
            s0 = jnp.zeros((b, RET_HEADS, RET_DK, RET_DV), state_ret.dtype)
            xp, sp = retention_layer(xp, pos_p, s0, ret_norm[j], ret_w_in[j], ret_w_out[j], log_gamma, N_META)
            xs, ss = retention_layer(xs, pos_s, state_ret[j], ret_norm[j], ret_w_in[j], ret_w_out[j], log_gamma, 0)
            ret_p.append(sp)
            ret_s.append(ss)
        xp = mlp_layer(xp, mlp_norm[i], mlp_w_up[i], mlp_w_down[i])
        xs = mlp_layer(xs, mlp_norm[i], mlp_w_up[i], mlp_w_down[i])
    y_prompt = rmsnorm(xp, final_norm)[:, N_META:]
    y_sample = rmsnorm(xs, final_norm)
    new_pool_prompt = jnp.stack(pool_p, axis=0)
    new_pool_sample = jnp.stack(pool_s, axis=0)
    new_ret_prompt = jnp.stack(ret_p, axis=0)
    new_ret_sample = jnp.stack(ret_s, axis=0)
    return (y_prompt, y_sample, new_pool_prompt, new_pool_sample, new_ret_prompt, new_ret_sample)
```

```python
import functools

import numpy as np
import jax
import jax.numpy as jnp
from jax import lax
from jax.experimental import pallas as pl
from jax.experimental.pallas import tpu as pltpu

D_MODEL = 2048
BATCH = 4
SEQ = 2048
DEPTH = 4
DEC_BATCH = 128
PAST_LEN = 16384
N_META = 16
POOL_WINDOWS = (2, 4, 8, 16)
POOL_GROUP_DIM = D_MODEL // len(POOL_WINDOWS)
POOL_BUF = max(POOL_WINDOWS) - 1
RET_HEADS = 8
RET_DK = D_MODEL // RET_HEADS
RET_DV = 2 * RET_DK
RET_VDIM = RET_HEADS * RET_DV
RET_IN = 2 * D_MODEL + 2 * RET_VDIM
RET_CHUNK = 128
ROPE_BASE = 10000.0
D_FF = 4 * D_MODEL
EPS = 1e-6

SMALL_ROWS = 256
N_ROWS = SMALL_ROWS + BATCH * SEQ
ROW_TILE = N_ROWS // 8
POOL_TILE = 256
POOL_TILES_PER_BATCH = SEQ // POOL_TILE
FF_TILE = 512
PROJ_TILE = 1024
OUT_TILE = 512
SAMPLE_BLOCK = 8
VMEM_LIMIT = 56 * 1024 * 1024

F32 = jnp.float32
BF16 = jnp.bfloat16


def _rmsnorm(x, g):
    return x * lax.rsqrt(jnp.mean(x * x, axis=-1, keepdims=True) + EPS) * g


def _params(*semantics):
    return pltpu.CompilerParams(dimension_semantics=semantics, vmem_limit_bytes=VMEM_LIMIT)


def _poolsum_kernel(s_ref, o_ref):
    row = lax.broadcasted_iota(jnp.int32, (POOL_BUF, 1), 0)
    for n in range(SAMPLE_BLOCK):
        s = s_ref[0, n]
        for gi, w in enumerate(POOL_WINDOWS):
            sl = slice(gi * POOL_GROUP_DIM, (gi + 1) * POOL_GROUP_DIM)
            part = jnp.where(row >= POOL_BUF - (w - 1), s[:, sl], 0.0)
            o_ref[0, n, :, sl] = jnp.sum(part, axis=0, keepdims=True)


def _pool_carry_sums(state_pool):
    n_layers = state_pool.shape[0]
    return pl.pallas_call(
        _poolsum_kernel,
        grid=(n_layers, DEC_BATCH // SAMPLE_BLOCK),
        in_specs=[pl.BlockSpec((1, SAMPLE_BLOCK, POOL_BUF, D_MODEL), lambda j, n: (j, n, 0, 0))],
        out_specs=pl.BlockSpec((1, SAMPLE_BLOCK, 1, D_MODEL), lambda j, n: (j, n, 0, 0)),
        out_shape=jax.ShapeDtypeStruct((n_layers, DEC_BATCH, 1, D_MODEL), F32),
        compiler_params=_params("arbitrary", "arbitrary"),
        name="pool_carry_sums",
    )(state_pool)


def _pool_kernel(x_ref, carry_ref, g_ref, sc_ref, w_ref, o_ref, usmall_ref, utail_ref, ext_scr, meta_scr):
    t = pl.program_id(0)
    is_small = t == 0
    chunk = lax.rem(jnp.maximum(t - 1, 0), POOL_TILES_PER_BATCH)
    x = x_ref[...]
    u = _rmsnorm(x, g_ref[...])

    @pl.when(is_small)
    def _():
        ext_scr[0:16, :] = jnp.zeros((16, D_MODEL), F32)
        meta_scr[...] = u[0:16, :]
        usmall_ref[...] = u

    @pl.when(jnp.logical_and(t > 0, chunk == 0))
    def _():
        ext_scr[0:16, :] = meta_scr[...]

    ext_scr[16:16 + POOL_TILE, :] = u

    row = lax.broadcasted_iota(jnp.int32, (POOL_TILE, 1), 0)
    meta_row = jnp.logical_and(is_small, row < N_META)
    sample_row = jnp.logical_and(is_small, row >= N_META)
    for gi, w in enumerate(POOL_WINDOWS):
        sl = slice(gi * POOL_GROUP_DIM, (gi + 1) * POOL_GROUP_DIM)
        ug = u[:, sl]
        ws = ug
        for k in range(1, w):
            ws = ws + ext_scr[16 - k:16 - k + POOL_TILE, sl]
        ws = jnp.where(sample_row, carry_ref[:, sl] + ug, ws)
        cnt = jnp.where(meta_row, jnp.minimum(w, row + 1), w).astype(F32)
        d = (ws / cnt - ug).astype(BF16)
        y = jnp.dot(d, w_ref[gi], preferred_element_type=F32)
        o_ref[:, sl] = x[:, sl] + y * sc_ref[:, sl]

    tail = u[POOL_TILE - 16:, :]
    ext_scr[0:16, :] = tail

    @pl.when(jnp.logical_and(t > 0, chunk == POOL_TILES_PER_BATCH - 1))
    def _():
        utail_ref[0] = tail


def _pool_layer(x, carry_ext, norm_g, scale, w_bf16):
    n_tiles = N_ROWS // POOL_TILE
    row_spec = pl.BlockSpec((POOL_TILE, D_MODEL), lambda t: (t, 0))
    vec_spec = pl.BlockSpec((1, D_MODEL), lambda t: (0, 0))
    return pl.pallas_call(
        _pool_kernel,
        grid=(n_tiles,),
        in_specs=[
            row_spec,
            pl.BlockSpec((POOL_TILE, D_MODEL), lambda t: (0, 0)),
            vec_spec,
            vec_spec,
            pl.BlockSpec((len(POOL_WINDOWS), POOL_GROUP_DIM, POOL_GROUP_DIM), lambda t: (0, 0, 0)),
        ],
        out_specs=[
            row_spec,
            pl.BlockSpec((POOL_TILE, D_MODEL), lambda t: (0, 0)),
            pl.BlockSpec((1, 16, D_MODEL), lambda t: (jnp.maximum(t - 1, 0) // POOL_TILES_PER_BATCH, 0, 0)),
        ],
        out_shape=[
            jax.ShapeDtypeStruct((N_ROWS, D_MODEL), F32),
            jax.ShapeDtypeStruct((SMALL_ROWS, D_MODEL), F32),
            jax.ShapeDtypeStruct((BATCH, 16, D_MODEL), F32),
        ],
        scratch_shapes=[pltpu.VMEM((16 + POOL_TILE, D_MODEL), F32), pltpu.VMEM((16, D_MODEL), F32)],
        compiler_params=_params("arbitrary"),
        name="pool_layer",
    )(x, carry_ext, norm_g.reshape(1, D_MODEL), scale.reshape(1, D_MODEL), w_bf16)


def _mlp_kernel(x_ref, g_ref, wu_ref, wd_ref, gf_ref, o_ref, u_scr, *, final_norm):
    j = pl.program_id(1)

    @pl.when(j == 0)
    def _():
        x = x_ref[...]
        u_scr[...] = _rmsnorm(x, g_ref[...]).astype(BF16)
        o_ref[...] = x

    h = jnp.dot(u_scr[...], wu_ref[...], preferred_element_type=F32)
    h = jnp.square(jnp.maximum(h, 0.0)).astype(BF16)
    for n in range(D_MODEL // OUT_TILE):
        sl = slice(n * OUT_TILE, (n + 1) * OUT_TILE)
        o_ref[:, sl] += jnp.dot(h, wd_ref[:, sl], preferred_element_type=F32)

    if final_norm:
        @pl.when(j == pl.num_programs(1) - 1)
        def _():
            o_ref[...] = _rmsnorm(o_ref[...], gf_ref[...])


def _mlp_layer(x, norm_g, w_up, w_down, final_g, final_norm):
    row_spec = pl.BlockSpec((ROW_TILE, D_MODEL), lambda i, j: (i, 0))
    vec_spec = pl.BlockSpec((1, D_MODEL), lambda i, j: (0, 0))
    return pl.pallas_call(
        functools.partial(_mlp_kernel, final_norm=final_norm),
        grid=(N_ROWS // ROW_TILE, D_FF // FF_TILE),
        in_specs=[
            pl.BlockSpec((ROW_TILE, D_MODEL), lambda i, j: (i, 0), pipeline_mode=pl.Buffered(1)),
            vec_spec,
            pl.BlockSpec((D_MODEL, FF_TILE), lambda i, j: (0, j)),
            pl.BlockSpec((FF_TILE, D_MODEL), lambda i, j: (j, 0)),
            vec_spec,
        ],
        out_specs=row_spec,
        out_shape=jax.ShapeDtypeStruct((N_ROWS, D_MODEL), F32),
        scratch_shapes=[pltpu.VMEM((ROW_TILE, D_MODEL), BF16)],
        compiler_params=_params("arbitrary", "arbitrary"),
        name="mlp_layer",
    )(x, norm_g.reshape(1, D_MODEL), w_up, w_down, final_g.reshape(1, D_MODEL))


def _proj_kernel(x_ref, g_ref, w_ref, cos_ref, sin_ref, o_ref, u_scr):
    j = pl.program_id(1)
    n_rope_tiles = 2 * D_MODEL // PROJ_TILE

    @pl.when(j == 0)
    def _():
        u_scr[...] = _rmsnorm(x_ref[...], g_ref[...]).astype(BF16)

    acc = jnp.dot(u_scr[...], w_ref[...], preferred_element_type=F32)

    @pl.when(j < n_rope_tiles)
    def _():
        k_scale = jnp.where(j >= n_rope_tiles // 2, RET_DK ** -0.5, 1.0).astype(F32)
        cos = cos_ref[...] * k_scale
        sin = sin_ref[...] * k_scale
        half = RET_DK // 2
        for hh in range(PROJ_TILE // RET_DK):
            x1 = acc[:, hh * RET_DK:hh * RET_DK + half]
            x2 = acc[:, hh * RET_DK + half:(hh + 1) * RET_DK]
            o_ref[:, hh * RET_DK:hh * RET_DK + half] = (x1 * cos - x2 * sin).astype(BF16)
            o_ref[:, hh * RET_DK + half:(hh + 1) * RET_DK] = (x1 * sin + x2 * cos).astype(BF16)

    @pl.when(j >= n_rope_tiles)
    def _():
        o_ref[...] = acc.astype(BF16)


def _ret_proj(x, norm_g, w_in, cos, sin):
    vec_spec = pl.BlockSpec((1, D_MODEL), lambda i, j: (0, 0))
    rope_spec = pl.BlockSpec((ROW_TILE, RET_DK // 2), lambda i, j: (i, 0))
    return pl.pallas_call(
        _proj_kernel,
        grid=(N_ROWS // ROW_TILE, RET_IN // PROJ_TILE),
        in_specs=[
            pl.BlockSpec((ROW_TILE, D_MODEL), lambda i, j: (i, 0), pipeline_mode=pl.Buffered(1)),
            vec_spec,
            pl.BlockSpec((D_MODEL, PROJ_TILE), lambda i, j: (0, j)),
            rope_spec,
            rope_spec,
        ],
        out_specs=pl.BlockSpec((ROW_TILE, PROJ_TILE), lambda i, j: (i, j)),
        out_shape=jax.ShapeDtypeStruct((N_ROWS, RET_IN), BF16),
        scratch_shapes=[pltpu.VMEM((ROW_TILE, D_MODEL), BF16)],
        compiler_params=_params("arbitrary", "arbitrary"),
        name="ret_proj",
    )(x, norm_g.reshape(1, D_MODEL), w_in, cos, sin)


def _head_norm_gate(o, g):
    mu = jnp.mean(o, axis=-1, keepdims=True)
    oc = o - mu
    var = jnp.mean(oc * oc, axis=-1, keepdims=True)
    return oc * lax.rsqrt(var + EPS) * (g * jax.nn.sigmoid(g))


def _ret_chunk_kernel(*refs, chunk, valid, n_chunks, n_pre):
    if n_pre:
        lg_ref, q_ref, k_ref, v_ref, g_ref, s0_ref, pre_ref, y_ref, sout_ref, s_scr = refs
    else:
        lg_ref, q_ref, k_ref, v_ref, g_ref, s0_ref, y_ref, sout_ref, s_scr = refs
    s = pl.program_id(0)
    c = lax.rem(jnp.maximum(s - n_pre, 0), n_chunks)

    if n_pre:
        @pl.when(s < n_pre)
        def _():
            y_ref[...] = pre_ref[...]

    @pl.when(s >= n_pre)
    def _():
        @pl.when(c == 0)
        def _():
            s_scr[...] = s0_ref[...]

        row = lax.broadcasted_iota(jnp.int32, (chunk, 1), 0).astype(F32)
        col = lax.broadcasted_iota(jnp.int32, (1, chunk), 1).astype(F32)
        diff = row - col
        for h in range(RET_HEADS):
            lg = lg_ref[h]
            qh = q_ref[:, h * RET_DK:(h + 1) * RET_DK]
            kh = k_ref[:, h * RET_DK:(h + 1) * RET_DK]
            vh = v_ref[:, h * RET_DV:(h + 1) * RET_DV]
            decay = jnp.where(diff >= 0, jnp.exp(jnp.maximum(diff, 0.0) * lg), 0.0)
            scores = lax.dot_general(qh, kh, (((1,), (1,)), ((), ())), preferred_element_type=F32) * decay
            inner = jnp.dot(scores.astype(BF16), vh, preferred_element_type=F32)
            st = s_scr[h]
            cross = jnp.dot(qh, st.astype(BF16), preferred_element_type=F32) * jnp.exp((row + 1.0) * lg)
            o = inner + cross
            kd = kh.astype(F32) * jnp.exp((valid - 1.0 - row) * lg)
            if valid < chunk:
                kd = jnp.where(row < valid, kd, 0.0)
            kd = kd.astype(BF16)
            s_dec = jnp.exp(jnp.full((1, RET_DV), valid, F32) * lg)
            s_scr[h] = st * s_dec + lax.dot_general(kd, vh, (((0,), (0,)), ((), ())),
                                                   preferred_element_type=F32)
            gh = g_ref[:, h * RET_DV:(h + 1) * RET_DV].astype(F32)
            y_ref[:, h * RET_DV:(h + 1) * RET_DV] = _head_norm_gate(o, gh).astype(BF16)

        @pl.when(c == n_chunks - 1)
        def _():
            sout_ref[0] = s_scr[...]


def _ret_chunks(log_gamma, qkvg, s0, pre, *, n_seq, n_chunks, chunk, valid):
    n_pre = 0 if pre is None else pre.shape[0] // chunk
    n_rows = (n_pre + n_seq * n_chunks) * chunk
    k_col = 1
    v_col = 2 * D_MODEL // RET_VDIM
    in_specs = [
        pl.BlockSpec(memory_space=pltpu.SMEM),
        pl.BlockSpec((chunk, D_MODEL), lambda s: (s, 0)),
        pl.BlockSpec((chunk, D_MODEL), lambda s: (s, k_col)),
        pl.BlockSpec((chunk, RET_VDIM), lambda s: (s, v_col)),
        pl.BlockSpec((chunk, RET_VDIM), lambda s: (s, v_col + 1)),
        pl.BlockSpec((RET_HEADS, RET_DK, RET_DV), lambda s: (0, 0, 0)),
    ]
    args = [log_gamma, qkvg, qkvg, qkvg, qkvg, s0]
    if n_pre:
        in_specs.append(pl.BlockSpec((chunk, RET_VDIM), lambda s: (jnp.minimum(s, n_pre - 1), 0)))
        args.append(pre)
    return pl.pallas_call(
        functools.partial(_ret_chunk_kernel, chunk=chunk, valid=valid, n_chunks=n_chunks, n_pre=n_pre),
        grid=(n_pre + n_seq * n_chunks,),
        in_specs=in_specs,
        out_specs=[
            pl.BlockSpec((chunk, RET_VDIM), lambda s: (s, 0)),
            pl.BlockSpec((1, RET_HEADS, RET_DK, RET_DV),
                         lambda s: (jnp.maximum(s - n_pre, 0) // n_chunks, 0, 0, 0)),
        ],
        out_shape=[
            jax.ShapeDtypeStruct((n_rows, RET_VDIM), BF16),
            jax.ShapeDtypeStruct((n_seq, RET_HEADS, RET_DK, RET_DV), F32),
        ],
        scratch_shapes=[pltpu.VMEM((RET_HEADS, RET_DK, RET_DV), F32)],
        compiler_params=_params("arbitrary"),
        name="ret_chunks",
    )(*args)


def _ret_sample_kernel(*refs, aliased):
    if aliased:
        lg_ref, qt_ref, kt_ref, v_ref, g_ref, s_ref, _, y_ref, so_ref = refs
    else:
        lg_ref, qt_ref, kt_ref, v_ref, g_ref, s_ref, y_ref, so_ref = refs
    h = pl.program_id(0)
    nb = pl.program_id(1)
    lg = lg_ref[h]
    gamma = jnp.exp(jnp.full((1, 128), 1.0, F32) * lg)
    sample = lax.broadcasted_iota(jnp.int32, (DEC_BATCH, 128), 0)
    row8 = lax.broadcasted_iota(jnp.int32, (SAMPLE_BLOCK, 1), 0)
    qt = qt_ref[0]
    kt = kt_ref[0]
    o_all = jnp.zeros((SAMPLE_BLOCK, RET_DV), F32)
    for n in range(SAMPLE_BLOCK):
        onehot = (sample == nb * SAMPLE_BLOCK + n).astype(BF16)
        qb = jnp.dot(qt, onehot, preferred_element_type=F32)
        kb = jnp.dot(kt, onehot, preferred_element_type=F32)
        v_row = v_ref[n:n + 1, :]
        parts = []
        for vt in range(RET_DV // 128):
            sl = slice(vt * 128, (vt + 1) * 128)
            s_new = s_ref[0, n, 0, :, sl] * gamma + kb * v_row[:, sl]
            so_ref[0, n, 0, :, sl] = s_new
            parts.append(jnp.sum(s_new * qb, axis=0, keepdims=True))
        o_n = jnp.concatenate(parts, axis=1)
        o_all = jnp.where(row8 == n, o_n, o_all)
    y_ref[...] = _head_norm_gate(o_all, g_ref[...])


def _ret_sample(log_gamma, qt, kt, v, g, state_ret, prev_out, layer):
    aliased = prev_out is not None
    state_spec = pl.BlockSpec((1, SAMPLE_BLOCK, 1, RET_DK, RET_DV), lambda h, n: (layer, n, h, 0, 0))
    t_spec = pl.BlockSpec((1, RET_DK, DEC_BATCH), lambda h, n: (h, 0, 0))
    vg_spec = pl.BlockSpec((SAMPLE_BLOCK, RET_DV), lambda h, n: (n, h))
    in_specs = [pl.BlockSpec(memory_space=pltpu.SMEM), t_spec, t_spec, vg_spec, vg_spec, state_spec]
    args = [log_gamma, qt, kt, v, g, state_ret]
    if aliased:
        in_specs.append(pl.BlockSpec(memory_space=pl.ANY))
        args.append(prev_out)
    return pl.pallas_call(
        functools.partial(_ret_sample_kernel, aliased=aliased),
        grid=(RET_HEADS, DEC_BATCH // SAMPLE_BLOCK),
        in_specs=in_specs,
        out_specs=[vg_spec, state_spec],
        out_shape=[
            jax.ShapeDtypeStruct((DEC_BATCH, RET_VDIM), F32),
            jax.ShapeDtypeStruct(state_ret.shape, F32),
        ],
        input_output_aliases={6: 1} if aliased else {},
        compiler_params=_params("arbitrary", "arbitrary"),
        name="ret_sample",
    )(*args)


def _outproj_kernel(x_ref, y_ref, w_ref, o_ref):
    o_ref[...] = x_ref[...] + jnp.dot(y_ref[...], w_ref[...], preferred_element_type=F32)


def _ret_outproj(x, y, w_out):
    x_spec = pl.BlockSpec((ROW_TILE, OUT_TILE), lambda i, n: (i, n))
    return pl.pallas_call(
        _outproj_kernel,
        grid=(N_ROWS // ROW_TILE, D_MODEL // OUT_TILE),
        in_specs=[
            x_spec,
            pl.BlockSpec((ROW_TILE, RET_VDIM), lambda i, n: (i, 0)),
            pl.BlockSpec((RET_VDIM, OUT_TILE), lambda i, n: (0, n)),
        ],
        out_specs=x_spec,
        out_shape=jax.ShapeDtypeStruct((N_ROWS, D_MODEL), F32),
        compiler_params=_params("arbitrary", "arbitrary"),
        name="ret_outproj",
    )(x, y, w_out)


def _rope_tables():
    pos = np.zeros((N_ROWS,), np.int32)
    pos[:N_META] = np.arange(N_META)
    pos[N_META:N_META + DEC_BATCH] = PAST_LEN
    pos[SMALL_ROWS:] = np.tile(N_META + np.arange(SEQ), BATCH)
    half = RET_DK // 2
    inv = ROPE_BASE ** (-jnp.arange(half, dtype=F32) / half)
    ang = jnp.asarray(pos).astype(F32)[:, None] * inv[None, :]
    return jnp.cos(ang), jnp.sin(ang)


def kernel(x_prompt, x_sample, state_pool, state_ret, meta_tokens, pool_norm, pool_w, pool_scale,
           ret_norm, ret_w_in, ret_w_out, mlp_norm, mlp_w_up, mlp_w_down, final_norm):
    assert x_prompt.shape == (BATCH, SEQ, D_MODEL) and x_sample.shape == (DEC_BATCH, 1, D_MODEL)
    x = jnp.concatenate([
        meta_tokens.astype(F32),
        x_sample.reshape(DEC_BATCH, D_MODEL),
        jnp.zeros((SMALL_ROWS - N_META - DEC_BATCH, D_MODEL), F32),
        x_prompt.reshape(BATCH * SEQ, D_MODEL),
    ], axis=0)
    log_gamma = jnp.log1p(-jnp.exp2(-5.0 - jnp.arange(RET_HEADS, dtype=F32)))
    cos, sin = _rope_tables()
    carry_sums = _pool_carry_sums(state_pool)
    pad_rows = SMALL_ROWS - N_META - DEC_BATCH

    pool_p, pool_s, ret_p = [], [], []
    ret_s = None
    for i in range(DEPTH):
        j = i // 2
        if i % 2 == 0:
            carry_ext = jnp.pad(carry_sums[j].reshape(DEC_BATCH, D_MODEL), ((N_META, pad_rows), (0, 0)))
            x, u_small, u_tail = _pool_layer(x, carry_ext, pool_norm[j], pool_scale[j], pool_w[j].astype(BF16))
            pool_p.append(u_tail[:, 1:])
            pool_s.append(jnp.concatenate(
                [state_pool[j][:, 1:], u_small[N_META:N_META + DEC_BATCH, None, :]], axis=1))
        else:
            qkvg = _ret_proj(x, ret_norm[j], ret_w_in[j].astype(BF16), cos, sin)
            zero_state = jnp.zeros((RET_HEADS, RET_DK, RET_DV), F32)
            y_meta, s_meta = _ret_chunks(log_gamma, qkvg, zero_state, None,
                                         n_seq=1, n_chunks=1, chunk=RET_CHUNK, valid=N_META)
            qkvg_s = qkvg[N_META:N_META + DEC_BATCH]
            qt = qkvg_s[:, :D_MODEL].reshape(DEC_BATCH, RET_HEADS, RET_DK).transpose(1, 2, 0)
            kt = qkvg_s[:, D_MODEL:2 * D_MODEL].reshape(DEC_BATCH, RET_HEADS, RET_DK).transpose(1, 2, 0)
            y_s, ret_s = _ret_sample(log_gamma, qt, kt,
                                     qkvg_s[:, 2 * D_MODEL:2 * D_MODEL + RET_VDIM].astype(F32),
                                     qkvg_s[:, 2 * D_MODEL + RET_VDIM:].astype(F32), state_ret, ret_s, j)
            y_small = jnp.concatenate(
                [y_meta[:N_META], y_s.astype(BF16), jnp.zeros((pad_rows, RET_VDIM), BF16)], axis=0)
            y, s_real = _ret_chunks(log_gamma, qkvg, s_meta[0], y_small, n_seq=BATCH,
                                    n_chunks=SEQ // RET_CHUNK, chunk=RET_CHUNK, valid=RET_CHUNK)
            ret_p.append(s_real)
            x = _ret_outproj(x, y, ret_w_out[j].astype(BF16))
        x = _mlp_layer(x, mlp_norm[i], mlp_w_up[i].astype(BF16), mlp_w_down[i].astype(BF16),
                       final_norm, i == DEPTH - 1)

    y_prompt = x[SMALL_ROWS:].reshape(BATCH, SEQ, D_MODEL)
    y_sample = x[N_META:N_META + DEC_BATCH].reshape(DEC_BATCH, 1, D_MODEL)
    return (y_prompt, y_sample, jnp.stack(pool_p, axis=0), jnp.stack(pool_s, axis=0),
            jnp.stack(ret_p, axis=0), ret_s)
```

```python
import functools

import numpy as np
import jax
import jax.numpy as jnp
from jax import lax
from jax.experimental import pallas as pl
from jax.experimental.pallas import tpu as pltpu

D_MODEL = 2048
BATCH = 4
SEQ = 2048
DEPTH = 4
DEC_BATCH = 128
PAST_LEN = 16384
N_META = 16
POOL_WINDOWS = (2, 4, 8, 16)
POOL_GROUP_DIM = D_MODEL // len(POOL_WINDOWS)
POOL_BUF = max(POOL_WINDOWS) - 1
RET_HEADS = 8
RET_DK = D_MODEL // RET_HEADS
RET_DV = 2 * RET_DK
RET_VDIM = RET_HEADS * RET_DV
RET_IN = 2 * D_MODEL + 2 * RET_VDIM
RET_CHUNK = 128
ROPE_BASE = 10000.0
D_FF = 4 * D_MODEL
EPS = 1e-6

SMALL_ROWS = 256
N_ROWS = SMALL_ROWS + BATCH * SEQ
ROW_TILE = N_ROWS // 8
POOL_TILE = 256
POOL_TILES_PER_BATCH = SEQ // POOL_TILE
FF_TILE = 512
PROJ_TILE = 1024
OUT_TILE = 512
SAMPLE_BLOCK = 8
VMEM_LIMIT = 56 * 1024 * 1024

F32 = jnp.float32
BF16 = jnp.bfloat16


def _rmsnorm(x, g):
    return x * lax.rsqrt(jnp.mean(x * x, axis=-1, keepdims=True) + EPS) * g


def _params(*semantics):
    return pltpu.CompilerParams(dimension_semantics=semantics, vmem_limit_bytes=VMEM_LIMIT)


def _poolsum_kernel(s_ref, o_ref):
    row = lax.broadcasted_iota(jnp.int32, (POOL_BUF, 1), 0)
    for n in range(SAMPLE_BLOCK):
        s = s_ref[0, n]
        for gi, w in enumerate(POOL_WINDOWS):
            sl = slice(gi * POOL_GROUP_DIM, (gi + 1) * POOL_GROUP_DIM)
            part = jnp.where(row >= POOL_BUF - (w - 1), s[:, sl], 0.0)
            o_ref[0, n, :, sl] = jnp.sum(part, axis=0, keepdims=True)


def _pool_carry_sums(state_pool):
    n_layers = state_pool.shape[0]
    return pl.pallas_call(
        _poolsum_kernel,
        grid=(n_layers, DEC_BATCH // SAMPLE_BLOCK),
        in_specs=[pl.BlockSpec((1, SAMPLE_BLOCK, POOL_BUF, D_MODEL), lambda j, n: (j, n, 0, 0))],
        out_specs=pl.BlockSpec((1, SAMPLE_BLOCK, 1, D_MODEL), lambda j, n: (j, n, 0, 0)),
        out_shape=jax.ShapeDtypeStruct((n_layers, DEC_BATCH, 1, D_MODEL), F32),
        compiler_params=_params("arbitrary", "arbitrary"),
        name="pool_carry_sums",
    )(state_pool)


def _pool_kernel(xs_ref, xr_ref, carry_ref, g_ref, sc_ref, w_ref, o_ref, usmall_ref, utail_ref,
                 prev_scr, meta_scr, w_scr):
    t = pl.program_id(0)
    is_small = t == 0
    chunk = lax.rem(jnp.maximum(t - 1, 0), POOL_TILES_PER_BATCH)
    x = jnp.where(is_small, xs_ref[...], xr_ref[...])
    u = _rmsnorm(x, g_ref[...])

    @pl.when(is_small)
    def _():
        prev_scr[...] = jnp.zeros((16, D_MODEL), F32)
        meta_scr[...] = u[0:16, :]
        usmall_ref[...] = u
        w_scr[...] = w_ref[0].astype(BF16)

    @pl.when(jnp.logical_and(t > 0, chunk == 0))
    def _():
        prev_scr[...] = meta_scr[...]

    ext = jnp.concatenate([prev_scr[...], u], axis=0)
    row = lax.broadcasted_iota(jnp.int32, (POOL_TILE, 1), 0)
    meta_row = jnp.logical_and(is_small, row < N_META)
    sample_row = jnp.logical_and(is_small, row >= N_META)
    for gi, w in enumerate(POOL_WINDOWS):
        sl = slice(gi * POOL_GROUP_DIM, (gi + 1) * POOL_GROUP_DIM)
        s = ext[:, sl]
        shift = 1
        while shift < w:
            s = s + pltpu.roll(s, shift, 0)
            shift *= 2
        ug = u[:, sl]
        ws = jnp.where(sample_row, carry_ref[:, sl] + ug, s[16:, :])
        inv_cnt = 1.0 / jnp.where(meta_row, jnp.minimum(w, row + 1), w).astype(F32)
        d = (ws * inv_cnt - ug).astype(BF16)
        y = jnp.dot(d, w_scr[gi], preferred_element_type=F32)
        o_ref[:, sl] = x[:, sl] + y * sc_ref[:, sl]

    tail = u[POOL_TILE - 16:, :]
    prev_scr[...] = tail

    @pl.when(jnp.logical_and(t > 0, chunk == POOL_TILES_PER_BATCH - 1))
    def _():
        utail_ref[0] = tail


def _pool_layer(x_small, x_real, real_block0, carry_ext, norm_g, scale, pool_w, layer):
    n_tiles = N_ROWS // POOL_TILE
    row_spec = pl.BlockSpec((POOL_TILE, D_MODEL), lambda t: (t, 0))
    first_spec = pl.BlockSpec((POOL_TILE, D_MODEL), lambda t: (0, 0))
    vec_spec = pl.BlockSpec((1, D_MODEL), lambda t: (0, 0))
    n_groups = len(POOL_WINDOWS)
    return pl.pallas_call(
        _pool_kernel,
        grid=(n_tiles,),
        in_specs=[
            first_spec,
            pl.BlockSpec((POOL_TILE, D_MODEL), lambda t: (jnp.maximum(t - 1, 0) + real_block0, 0)),
            first_spec,
            vec_spec,
            vec_spec,
            pl.BlockSpec((1, n_groups, POOL_GROUP_DIM, POOL_GROUP_DIM), lambda t: (layer, 0, 0, 0)),
        ],
        out_specs=[
            row_spec,
            first_spec,
            pl.BlockSpec((1, 16, D_MODEL), lambda t: (jnp.maximum(t - 1, 0) // POOL_TILES_PER_BATCH, 0, 0)),
        ],
        out_shape=[
            jax.ShapeDtypeStruct((N_ROWS, D_MODEL), F32),
            jax.ShapeDtypeStruct((SMALL_ROWS, D_MODEL), F32),
            jax.ShapeDtypeStruct((BATCH, 16, D_MODEL), F32),
        ],
        scratch_shapes=[
            pltpu.VMEM((16, D_MODEL), F32),
            pltpu.VMEM((16, D_MODEL), F32),
            pltpu.VMEM((n_groups, POOL_GROUP_DIM, POOL_GROUP_DIM), BF16),
        ],
        compiler_params=_params("arbitrary"),
        name="pool_layer",
    )(x_small, x_real, carry_ext, norm_g.reshape(1, D_MODEL), scale.reshape(1, D_MODEL), pool_w)


def _mlp_kernel(x_ref, g_ref, wu_ref, wd_ref, gf_ref, o_ref, u_scr, *, final_norm):
    j = pl.program_id(1)

    @pl.when(j == 0)
    def _():
        x = x_ref[...]
        u_scr[...] = _rmsnorm(x, g_ref[...]).astype(BF16)
        o_ref[...] = x

    h = jnp.dot(u_scr[...], wu_ref[0].astype(BF16), preferred_element_type=F32)
    h = jnp.square(jnp.maximum(h, 0.0)).astype(BF16)
    for n in range(D_MODEL // OUT_TILE):
        sl = slice(n * OUT_TILE, (n + 1) * OUT_TILE)
        o_ref[:, sl] += jnp.dot(h, wd_ref[0, :, sl].astype(BF16), preferred_element_type=F32)

    if final_norm:
        @pl.when(j == pl.num_programs(1) - 1)
        def _():
            o_ref[...] = _rmsnorm(o_ref[...], gf_ref[...])


def _mlp_layer(x, norm_g, w_up, w_down, layer, final_g, final_norm):
    row_spec = pl.BlockSpec((ROW_TILE, D_MODEL), lambda i, j: (i, 0))
    vec_spec = pl.BlockSpec((1, D_MODEL), lambda i, j: (0, 0))
    return pl.pallas_call(
        functools.partial(_mlp_kernel, final_norm=final_norm),
        grid=(N_ROWS // ROW_TILE, D_FF // FF_TILE),
        in_specs=[
            pl.BlockSpec((ROW_TILE, D_MODEL), lambda i, j: (i, 0), pipeline_mode=pl.Buffered(1)),
            vec_spec,
            pl.BlockSpec((1, D_MODEL, FF_TILE), lambda i, j: (layer, 0, j)),
            pl.BlockSpec((1, FF_TILE, D_MODEL), lambda i, j: (layer, j, 0)),
            vec_spec,
        ],
        out_specs=row_spec,
        out_shape=jax.ShapeDtypeStruct((N_ROWS, D_MODEL), F32),
        scratch_shapes=[pltpu.VMEM((ROW_TILE, D_MODEL), BF16)],
        compiler_params=_params("arbitrary", "arbitrary"),
        name="mlp_layer",
    )(x, norm_g.reshape(1, D_MODEL), w_up, w_down, final_g.reshape(1, D_MODEL))


def _proj_kernel(x_ref, g_ref, w_ref, cos_ref, sin_ref, o_ref, u_scr):
    j = pl.program_id(1)
    n_rope_tiles = 2 * D_MODEL // PROJ_TILE

    @pl.when(j == 0)
    def _():
        u_scr[...] = _rmsnorm(x_ref[...], g_ref[...]).astype(BF16)

    acc = jnp.dot(u_scr[...], w_ref[0].astype(BF16), preferred_element_type=F32)

    @pl.when(j < n_rope_tiles)
    def _():
        k_scale = jnp.where(j >= n_rope_tiles // 2, RET_DK ** -0.5, 1.0).astype(F32)
        cos = cos_ref[...] * k_scale
        sin = sin_ref[...] * k_scale
        half = RET_DK // 2
        for hh in range(PROJ_TILE // RET_DK):
            x1 = acc[:, hh * RET_DK:hh * RET_DK + half]
            x2 = acc[:, hh * RET_DK + half:(hh + 1) * RET_DK]
            o_ref[:, hh * RET_DK:hh * RET_DK + half] = (x1 * cos - x2 * sin).astype(BF16)
            o_ref[:, hh * RET_DK + half:(hh + 1) * RET_DK] = (x1 * sin + x2 * cos).astype(BF16)

    @pl.when(j >= n_rope_tiles)
    def _():
        o_ref[...] = acc.astype(BF16)


def _ret_proj(x, norm_g, w_in, layer, cos, sin):
    vec_spec = pl.BlockSpec((1, D_MODEL), lambda i, j: (0, 0))
    rope_spec = pl.BlockSpec((ROW_TILE, RET_DK // 2), lambda i, j: (i, 0))
    return pl.pallas_call(
        _proj_kernel,
        grid=(N_ROWS // ROW_TILE, RET_IN // PROJ_TILE),
        in_specs=[
            pl.BlockSpec((ROW_TILE, D_MODEL), lambda i, j: (i, 0), pipeline_mode=pl.Buffered(1)),
            vec_spec,
            pl.BlockSpec((1, D_MODEL, PROJ_TILE), lambda i, j: (layer, 0, j)),
            rope_spec,
            rope_spec,
        ],
        out_specs=pl.BlockSpec((ROW_TILE, PROJ_TILE), lambda i, j: (i, j)),
        out_shape=jax.ShapeDtypeStruct((N_ROWS, RET_IN), BF16),
        scratch_shapes=[pltpu.VMEM((ROW_TILE, D_MODEL), BF16)],
        compiler_params=_params("arbitrary", "arbitrary"),
        name="ret_proj",
    )(x, norm_g.reshape(1, D_MODEL), w_in, cos, sin)


def _head_norm_gate(o, g):
    mu = jnp.mean(o, axis=-1, keepdims=True)
    oc = o - mu
    var = jnp.mean(oc * oc, axis=-1, keepdims=True)
    return oc * lax.rsqrt(var + EPS) * (g * jax.nn.sigmoid(g))


def _ret_chunk_kernel(*refs, chunk, valid, n_chunks, n_pre):
    if n_pre:
        lg_ref, q_ref, k_ref, v_ref, g_ref, s0_ref, pre_ref, y_ref, sout_ref, s_scr = refs
    else:
        lg_ref, q_ref, k_ref, v_ref, g_ref, s0_ref, y_ref, sout_ref, s_scr = refs
    s = pl.program_id(0)
    c = lax.rem(jnp.maximum(s - n_pre, 0), n_chunks)

    if n_pre:
        @pl.when(s < n_pre)
        def _():
            y_ref[...] = pre_ref[...]

    @pl.when(s >= n_pre)
    def _():
        @pl.when(c == 0)
        def _():
            s_scr[...] = s0_ref[...]

        row = lax.broadcasted_iota(jnp.int32, (chunk, 1), 0).astype(F32)
        col = lax.broadcasted_iota(jnp.int32, (1, chunk), 1).astype(F32)
        diff = row - col
        for h in range(RET_HEADS):
            lg = lg_ref[h]
            qh = q_ref[:, h * RET_DK:(h + 1) * RET_DK]
            kh = k_ref[:, h * RET_DK:(h + 1) * RET_DK]
            vh = v_ref[:, h * RET_DV:(h + 1) * RET_DV]
            decay = jnp.where(diff >= 0, jnp.exp(jnp.maximum(diff, 0.0) * lg), 0.0)
            scores = lax.dot_general(qh, kh, (((1,), (1,)), ((), ())), preferred_element_type=F32) * decay
            inner = jnp.dot(scores.astype(BF16), vh, preferred_element_type=F32)
            st = s_scr[h]
            cross = jnp.dot(qh, st.astype(BF16), preferred_element_type=F32) * jnp.exp((row + 1.0) * lg)
            o = inner + cross
            kd = kh.astype(F32) * jnp.exp((valid - 1.0 - row) * lg)
            if valid < chunk:
                kd = jnp.where(row < valid, kd, 0.0)
            kd = kd.astype(BF16)
            s_dec = jnp.exp(jnp.full((1, RET_DV), valid, F32) * lg)
            s_scr[h] = st * s_dec + lax.dot_general(kd, vh, (((0,), (0,)), ((), ())),
                                                   preferred_element_type=F32)
            gh = g_ref[:, h * RET_DV:(h + 1) * RET_DV].astype(F32)
            y_ref[:, h * RET_DV:(h + 1) * RET_DV] = _head_norm_gate(o, gh).astype(BF16)

        @pl.when(c == n_chunks - 1)
        def _():
            sout_ref[0] = s_scr[...]


def _ret_chunks(log_gamma, qkvg, s0, pre, *, n_seq, n_chunks, chunk, valid):
    n_pre = 0 if pre is None else pre.shape[0] // chunk
    n_rows = (n_pre + n_seq * n_chunks) * chunk
    k_col = 1
    v_col = 2 * D_MODEL // RET_VDIM
    in_specs = [
        pl.BlockSpec(memory_space=pltpu.SMEM),
        pl.BlockSpec((chunk, D_MODEL), lambda s: (s, 0)),
        pl.BlockSpec((chunk, D_MODEL), lambda s: (s, k_col)),
        pl.BlockSpec((chunk, RET_VDIM), lambda s: (s, v_col)),
        pl.BlockSpec((chunk, RET_VDIM), lambda s: (s, v_col + 1)),
        pl.BlockSpec((RET_HEADS, RET_DK, RET_DV), lambda s: (0, 0, 0)),
    ]
    args = [log_gamma, qkvg, qkvg, qkvg, qkvg, s0]
    if n_pre:
        in_specs.append(pl.BlockSpec((chunk, RET_VDIM), lambda s: (jnp.minimum(s, n_pre - 1), 0)))
        args.append(pre)
    return pl.pallas_call(
        functools.partial(_ret_chunk_kernel, chunk=chunk, valid=valid, n_chunks=n_chunks, n_pre=n_pre),
        grid=(n_pre + n_seq * n_chunks,),
        in_specs=in_specs,
        out_specs=[
            pl.BlockSpec((chunk, RET_VDIM), lambda s: (s, 0)),
            pl.BlockSpec((1, RET_HEADS, RET_DK, RET_DV),
                         lambda s: (jnp.maximum(s - n_pre, 0) // n_chunks, 0, 0, 0)),
        ],
        out_shape=[
            jax.ShapeDtypeStruct((n_rows, RET_VDIM), BF16),
            jax.ShapeDtypeStruct((n_seq, RET_HEADS, RET_DK, RET_DV), F32),
        ],
        scratch_shapes=[pltpu.VMEM((RET_HEADS, RET_DK, RET_DV), F32)],
        compiler_params=_params("arbitrary"),
        name="ret_chunks",
    )(*args)


def _ret_sample_kernel(*refs, aliased):
    if aliased:
        lg_ref, qt_ref, kt_ref, v_ref, g_ref, s_ref, _, y_ref, so_ref = refs
    else:
        lg_ref, qt_ref, kt_ref, v_ref, g_ref, s_ref, y_ref, so_ref = refs
    h = pl.program_id(0)
    nb = pl.program_id(1)
    lg = lg_ref[h]
    gamma = jnp.exp(jnp.full((1, 128), 1.0, F32) * lg)
    sample = lax.broadcasted_iota(jnp.int32, (DEC_BATCH, 128), 0)
    row8 = lax.broadcasted_iota(jnp.int32, (SAMPLE_BLOCK, 1), 0)
    qt = qt_ref[0]
    kt = kt_ref[0]
    o_all = jnp.zeros((SAMPLE_BLOCK, RET_DV), F32)
    for n in range(SAMPLE_BLOCK):
        onehot = (sample == nb * SAMPLE_BLOCK + n).astype(BF16)
        qb = jnp.dot(qt, onehot, preferred_element_type=F32)
        kb = jnp.dot(kt, onehot, preferred_element_type=F32)
        v_row = v_ref[n:n + 1, :]
        parts = []
        for vt in range(RET_DV // 128):
            sl = slice(vt * 128, (vt + 1) * 128)
            s_new = s_ref[0, n, 0, :, sl] * gamma + kb * v_row[:, sl]
            so_ref[0, n, 0, :, sl] = s_new
            parts.append(jnp.sum(s_new * qb, axis=0, keepdims=True))
        o_n = jnp.concatenate(parts, axis=1)
        o_all = jnp.where(row8 == n, o_n, o_all)
    y_ref[...] = _head_norm_gate(o_all, g_ref[...])


def _ret_sample(log_gamma, qt, kt, v, g, state_ret, prev_out, layer):
    aliased = prev_out is not None
    state_spec = pl.BlockSpec((1, SAMPLE_BLOCK, 1, RET_DK, RET_DV), lambda h, n: (layer, n, h, 0, 0))
    t_spec = pl.BlockSpec((1, RET_DK, DEC_BATCH), lambda h, n: (h, 0, 0))
    vg_spec = pl.BlockSpec((SAMPLE_BLOCK, RET_DV), lambda h, n: (n, h))
    in_specs = [pl.BlockSpec(memory_space=pltpu.SMEM), t_spec, t_spec, vg_spec, vg_spec, state_spec]
    args = [log_gamma, qt, kt, v, g, state_ret]
    if aliased:
        in_specs.append(pl.BlockSpec(memory_space=pl.ANY))
        args.append(prev_out)
    return pl.pallas_call(
        functools.partial(_ret_sample_kernel, aliased=aliased),
        grid=(RET_HEADS, DEC_BATCH // SAMPLE_BLOCK),
        in_specs=in_specs,
        out_specs=[vg_spec, state_spec],
        out_shape=[
            jax.ShapeDtypeStruct((DEC_BATCH, RET_VDIM), F32),
            jax.ShapeDtypeStruct(state_ret.shape, F32),
        ],
        input_output_aliases={6: 1} if aliased else {},
        compiler_params=_params("arbitrary", "arbitrary"),
        name="ret_sample",
    )(*args)


def _outproj_kernel(x_ref, y_ref, w_ref, o_ref):
    o_ref[...] = x_ref[...] + jnp.dot(y_ref[...], w_ref[0].astype(BF16), preferred_element_type=F32)


def _ret_outproj(x, y, w_out, layer):
    x_spec = pl.BlockSpec((ROW_TILE, OUT_TILE), lambda i, n: (i, n))
    return pl.pallas_call(
        _outproj_kernel,
        grid=(N_ROWS // ROW_TILE, D_MODEL // OUT_TILE),
        in_specs=[
            x_spec,
            pl.BlockSpec((ROW_TILE, RET_VDIM), lambda i, n: (i, 0)),
            pl.BlockSpec((1, RET_VDIM, OUT_TILE), lambda i, n: (layer, 0, n)),
        ],
        out_specs=x_spec,
        out_shape=jax.ShapeDtypeStruct((N_ROWS, D_MODEL), F32),
        compiler_params=_params("arbitrary", "arbitrary"),
        name="ret_outproj",
    )(x, y, w_out)


def _rope_tables():
    pos = np.zeros((N_ROWS,), np.int32)
    pos[:N_META] = np.arange(N_META)
    pos[N_META:N_META + DEC_BATCH] = PAST_LEN
    pos[SMALL_ROWS:] = np.tile(N_META + np.arange(SEQ), BATCH)
    half = RET_DK // 2
    inv = ROPE_BASE ** (-jnp.arange(half, dtype=F32) / half)
    ang = jnp.asarray(pos).astype(F32)[:, None] * inv[None, :]
    return jnp.cos(ang), jnp.sin(ang)


def kernel(x_prompt, x_sample, state_pool, state_ret, meta_tokens, pool_norm, pool_w, pool_scale,
           ret_norm, ret_w_in, ret_w_out, mlp_norm, mlp_w_up, mlp_w_down, final_norm):
    assert x_prompt.shape == (BATCH, SEQ, D_MODEL) and x_sample.shape == (DEC_BATCH, 1, D_MODEL)
    pad_rows = SMALL_ROWS - N_META - DEC_BATCH
    x_small = jnp.concatenate([
        meta_tokens.astype(F32),
        x_sample.reshape(DEC_BATCH, D_MODEL),
        jnp.zeros((pad_rows, D_MODEL), F32),
    ], axis=0)
    x = None
    log_gamma = jnp.log1p(-jnp.exp2(-5.0 - jnp.arange(RET_HEADS, dtype=F32)))
    cos, sin = _rope_tables()
    carry_sums = _pool_carry_sums(state_pool)

    pool_p, pool_s, ret_p = [], [], []
    ret_s = None
    for i in range(DEPTH):
        j = i // 2
        if i % 2 == 0:
            carry_ext = jnp.pad(carry_sums[j].reshape(DEC_BATCH, D_MODEL), ((N_META, pad_rows), (0, 0)))
            if x is None:
                xs, xr, block0 = x_small, x_prompt.reshape(BATCH * SEQ, D_MODEL), 0
            else:
                xs, xr, block0 = x, x, SMALL_ROWS // POOL_TILE
            x, u_small, u_tail = _pool_layer(xs, xr, block0, carry_ext, pool_norm[j], pool_scale[j], pool_w, j)
            pool_p.append(u_tail[:, 1:])
            pool_s.append(jnp.concatenate(
                [state_pool[j][:, 1:], u_small[N_META:N_META + DEC_BATCH, None, :]], axis=1))
        else:
            qkvg = _ret_proj(x, ret_norm[j], ret_w_in, j, cos, sin)
            zero_state = jnp.zeros((RET_HEADS, RET_DK, RET_DV), F32)
            y_meta, s_meta = _ret_chunks(log_gamma, qkvg, zero_state, None,
                                         n_seq=1, n_chunks=1, chunk=RET_CHUNK, valid=N_META)
            qkvg_s = qkvg[N_META:N_META + DEC_BATCH]
            qt = qkvg_s[:, :D_MODEL].reshape(DEC_BATCH, RET_HEADS, RET_DK).transpose(1, 2, 0)
            kt = qkvg_s[:, D_MODEL:2 * D_MODEL].reshape(DEC_BATCH, RET_HEADS, RET_DK).transpose(1, 2, 0)
            y_s, ret_s = _ret_sample(log_gamma, qt, kt,
                                     qkvg_s[:, 2 * D_MODEL:2 * D_MODEL + RET_VDIM].astype(F32),
                                     qkvg_s[:, 2 * D_MODEL + RET_VDIM:].astype(F32), state_ret, ret_s, j)
            y_small = jnp.concatenate(
                [y_meta[:N_META], y_s.astype(BF16), jnp.zeros((pad_rows, RET_VDIM), BF16)], axis=0)
            y, s_real = _ret_chunks(log_gamma, qkvg, s_meta[0], y_small, n_seq=BATCH,
                                    n_chunks=SEQ // RET_CHUNK, chunk=RET_CHUNK, valid=RET_CHUNK)
            ret_p.append(s_real)
            x = _ret_outproj(x, y, ret_w_out, j)
        x = _mlp_layer(x, mlp_norm[i], mlp_w_up, mlp_w_down, i, final_norm, i == DEPTH - 1)

    y_prompt = x[SMALL_ROWS:].reshape(BATCH, SEQ, D_MODEL)
    y_sample = x[N_META:N_META + DEC_BATCH].reshape(DEC_BATCH, 1, D_MODEL)
    return (y_prompt, y_sample, jnp.stack(pool_p, axis=0), jnp.stack(pool_s, axis=0),
            jnp.stack(ret_p, axis=0), ret_s)
```

```python
import functools

import numpy as np
import jax
import jax.numpy as jnp
from jax import lax
from jax.experimental import pallas as pl
from jax.experimental.pallas import tpu as pltpu

D_MODEL = 2048
BATCH = 4
SEQ = 2048
DEPTH = 4
DEC_BATCH = 128
PAST_LEN = 16384
N_META = 16
POOL_WINDOWS = (2, 4, 8, 16)
POOL_GROUP_DIM = D_MODEL // len(POOL_WINDOWS)
POOL_BUF = max(POOL_WINDOWS) - 1
RET_HEADS = 8
RET_DK = D_MODEL // RET_HEADS
RET_DV = 2 * RET_DK
RET_VDIM = RET_HEADS * RET_DV
RET_IN = 2 * D_MODEL + 2 * RET_VDIM
RET_CHUNK = 128
ROPE_BASE = 10000.0
D_FF = 4 * D_MODEL
EPS = 1e-6

SMALL_ROWS = 256
N_ROWS = SMALL_ROWS + BATCH * SEQ
ROW_TILE = N_ROWS // 8
POOL_TILE = 256
POOL_TILES_PER_BATCH = SEQ // POOL_TILE
FF_TILE = 512
PROJ_TILE = 1024
OUT_TILE = 512
SAMPLE_BLOCK = 8
CHUNK_SPLIT = 2
VMEM_LIMIT = 60 * 1024 * 1024

F32 = jnp.float32
BF16 = jnp.bfloat16


def _rmsnorm(x, g):
    return x * lax.rsqrt(jnp.mean(x * x, axis=-1, keepdims=True) + EPS) * g


def _params(*semantics):
    return pltpu.CompilerParams(dimension_semantics=semantics, vmem_limit_bytes=VMEM_LIMIT)


def _poolsum_kernel(s_ref, o_ref):
    for gi, w in enumerate(POOL_WINDOWS):
        sl = slice(gi * POOL_GROUP_DIM, (gi + 1) * POOL_GROUP_DIM)
        acc = s_ref[0, POOL_BUF - 1, :, sl]
        for r in range(POOL_BUF - (w - 1), POOL_BUF - 1):
            acc = acc + s_ref[0, r, :, sl]
        o_ref[0, :, sl] = acc


def _pool_carry_sums(state_slots):
    n_layers = state_slots.shape[0]
    n_rows = 32
    return pl.pallas_call(
        _poolsum_kernel,
        grid=(n_layers, DEC_BATCH // n_rows),
        in_specs=[pl.BlockSpec((1, POOL_BUF, n_rows, D_MODEL), lambda j, n: (j, 0, n, 0))],
        out_specs=pl.BlockSpec((1, n_rows, D_MODEL), lambda j, n: (j, n, 0)),
        out_shape=jax.ShapeDtypeStruct((n_layers, DEC_BATCH, D_MODEL), F32),
        compiler_params=_params("arbitrary", "arbitrary"),
        name="pool_carry_sums",
    )(state_slots)


def _pool_state_kernel(s_ref, u0_ref, u1_ref, o_ref):
    r = pl.program_id(1)

    @pl.when(r < POOL_BUF - 1)
    def _():
        o_ref[0, 0] = s_ref[0, 0]

    @pl.when(r == POOL_BUF - 1)
    def _():
        u = jnp.where(pl.program_id(0) == 0, u0_ref[...], u1_ref[...])
        o_ref[0, 0] = u[N_META:N_META + DEC_BATCH, :]


def _pool_sample_states(state_slots, u_small0, u_small1):
    u_spec = pl.BlockSpec((SMALL_ROWS, D_MODEL), lambda j, r: (0, 0))
    return pl.pallas_call(
        _pool_state_kernel,
        grid=(state_slots.shape[0], POOL_BUF),
        in_specs=[
            pl.BlockSpec((1, 1, DEC_BATCH, D_MODEL), lambda j, r: (j, jnp.minimum(r + 1, POOL_BUF - 1), 0, 0)),
            u_spec,
            u_spec,
        ],
        out_specs=pl.BlockSpec((1, 1, DEC_BATCH, D_MODEL), lambda j, r: (j, r, 0, 0)),
        out_shape=jax.ShapeDtypeStruct(state_slots.shape, F32),
        compiler_params=_params("arbitrary", "arbitrary"),
        name="pool_sample_states",
    )(state_slots, u_small0, u_small1)


def _pool_kernel(xs_ref, xr_ref, carry_ref, g_ref, sc_ref, w_ref, o_ref, usmall_ref, utail_ref,
                 prev_scr, meta_scr, w_scr):
    t = pl.program_id(0)
    is_small = t == 0
    chunk = lax.rem(jnp.maximum(t - 1, 0), POOL_TILES_PER_BATCH)
    x = jnp.where(is_small, xs_ref[...], xr_ref[...])
    u = _rmsnorm(x, g_ref[...])

    @pl.when(is_small)
    def _():
        prev_scr[...] = jnp.zeros((16, D_MODEL), F32)
        meta_scr[...] = u[0:16, :]
        usmall_ref[...] = u
        w_scr[...] = w_ref[0].astype(BF16)

    @pl.when(jnp.logical_and(t > 0, chunk == 0))
    def _():
        prev_scr[...] = meta_scr[...]

    ext = jnp.concatenate([prev_scr[...], u], axis=0)
    row = lax.broadcasted_iota(jnp.int32, (POOL_TILE, 1), 0)
    meta_row = jnp.logical_and(is_small, row < N_META)
    sample_row = jnp.logical_and(is_small, row >= N_META)
    for gi, w in enumerate(POOL_WINDOWS):
        sl = slice(gi * POOL_GROUP_DIM, (gi + 1) * POOL_GROUP_DIM)
        s = ext[:, sl]
        shift = 1
        while shift < w:
            s = s + pltpu.roll(s, shift, 0)
            shift *= 2
        ug = u[:, sl]
        ws = jnp.where(sample_row, carry_ref[:, sl] + ug, s[16:, :])
        inv_cnt = 1.0 / jnp.where(meta_row, jnp.minimum(w, row + 1), w).astype(F32)
        d = (ws * inv_cnt - ug).astype(BF16)
        y = jnp.dot(d, w_scr[gi], preferred_element_type=F32)
        o_ref[:, sl] = x[:, sl] + y * sc_ref[:, sl]

    tail = u[POOL_TILE - 16:, :]
    prev_scr[...] = tail

    @pl.when(jnp.logical_and(t > 0, chunk == POOL_TILES_PER_BATCH - 1))
    def _():
        utail_ref[0] = tail


def _pool_layer(x_small, x_real, real_block0, carry_ext, norm_g, scale, pool_w, layer):
    n_tiles = N_ROWS // POOL_TILE
    row_spec = pl.BlockSpec((POOL_TILE, D_MODEL), lambda t: (t, 0))
    first_spec = pl.BlockSpec((POOL_TILE, D_MODEL), lambda t: (0, 0))
    vec_spec = pl.BlockSpec((1, D_MODEL), lambda t: (0, 0))
    n_groups = len(POOL_WINDOWS)
    return pl.pallas_call(
        _pool_kernel,
        grid=(n_tiles,),
        in_specs=[
            first_spec,
            pl.BlockSpec((POOL_TILE, D_MODEL), lambda t: (jnp.maximum(t - 1, 0) + real_block0, 0)),
            first_spec,
            vec_spec,
            vec_spec,
            pl.BlockSpec((1, n_groups, POOL_GROUP_DIM, POOL_GROUP_DIM), lambda t: (layer, 0, 0, 0)),
        ],
        out_specs=[
            row_spec,
            first_spec,
            pl.BlockSpec((1, 16, D_MODEL), lambda t: (jnp.maximum(t - 1, 0) // POOL_TILES_PER_BATCH, 0, 0)),
        ],
        out_shape=[
            jax.ShapeDtypeStruct((N_ROWS, D_MODEL), F32),
            jax.ShapeDtypeStruct((SMALL_ROWS, D_MODEL), F32),
            jax.ShapeDtypeStruct((BATCH, 16, D_MODEL), F32),
        ],
        scratch_shapes=[
            pltpu.VMEM((16, D_MODEL), F32),
            pltpu.VMEM((16, D_MODEL), F32),
            pltpu.VMEM((n_groups, POOL_GROUP_DIM, POOL_GROUP_DIM), BF16),
        ],
        compiler_params=_params("arbitrary"),
        name="pool_layer",
    )(x_small, x_real, carry_ext, norm_g.reshape(1, D_MODEL), scale.reshape(1, D_MODEL), pool_w)


def _mlp_kernel(x_ref, g_ref, wu_ref, wd_ref, gf_ref, o_ref, u_scr, *, final_norm):
    j = pl.program_id(1)

    @pl.when(j == 0)
    def _():
        x = x_ref[...]
        u_scr[...] = _rmsnorm(x, g_ref[...]).astype(BF16)
        o_ref[...] = x

    h = jnp.dot(u_scr[...], wu_ref[0].astype(BF16), preferred_element_type=F32)
    h = jnp.square(jnp.maximum(h, 0.0)).astype(BF16)
    for n in range(D_MODEL // OUT_TILE):
        sl = slice(n * OUT_TILE, (n + 1) * OUT_TILE)
        o_ref[:, sl] += jnp.dot(h, wd_ref[0, :, sl].astype(BF16), preferred_element_type=F32)

    if final_norm:
        @pl.when(j == pl.num_programs(1) - 1)
        def _():
            o_ref[...] = _rmsnorm(o_ref[...], gf_ref[...])


def _mlp_layer(x, norm_g, w_up, w_down, layer, final_g, final_norm):
    row_spec = pl.BlockSpec((ROW_TILE, D_MODEL), lambda i, j: (i, 0))
    vec_spec = pl.BlockSpec((1, D_MODEL), lambda i, j: (0, 0))
    return pl.pallas_call(
        functools.partial(_mlp_kernel, final_norm=final_norm),
        grid=(N_ROWS // ROW_TILE, D_FF // FF_TILE),
        in_specs=[
            pl.BlockSpec((ROW_TILE, D_MODEL), lambda i, j: (i, 0)),
            vec_spec,
            pl.BlockSpec((1, D_MODEL, FF_TILE), lambda i, j: (layer, 0, j)),
            pl.BlockSpec((1, FF_TILE, D_MODEL), lambda i, j: (layer, j, 0)),
            vec_spec,
        ],
        out_specs=row_spec,
        out_shape=jax.ShapeDtypeStruct((N_ROWS, D_MODEL), F32),
        scratch_shapes=[pltpu.VMEM((ROW_TILE, D_MODEL), BF16)],
        compiler_params=_params("arbitrary", "arbitrary"),
        name="mlp_layer",
    )(x, norm_g.reshape(1, D_MODEL), w_up, w_down, final_g.reshape(1, D_MODEL))


def _proj_kernel(x_ref, g_ref, w_ref, cos_ref, sin_ref, o_ref, u_scr):
    j = pl.program_id(1)
    n_rope_tiles = 2 * D_MODEL // PROJ_TILE

    @pl.when(j == 0)
    def _():
        u_scr[...] = _rmsnorm(x_ref[...], g_ref[...]).astype(BF16)

    is_rope = j < n_rope_tiles
    k_scale = jnp.where(j >= n_rope_tiles // 2, RET_DK ** -0.5, 1.0).astype(F32)
    cos = jnp.where(is_rope, cos_ref[...] * k_scale, 1.0)
    sin = jnp.where(is_rope, sin_ref[...] * k_scale, 0.0)
    half = RET_DK // 2
    for hh in range(PROJ_TILE // RET_DK):
        acc = jnp.dot(u_scr[...], w_ref[0, :, hh * RET_DK:(hh + 1) * RET_DK].astype(BF16),
                      preferred_element_type=F32)
        x1 = acc[:, :half]
        x2 = acc[:, half:]
        o_ref[:, hh * RET_DK:hh * RET_DK + half] = (x1 * cos - x2 * sin).astype(BF16)
        o_ref[:, hh * RET_DK + half:(hh + 1) * RET_DK] = (x1 * sin + x2 * cos).astype(BF16)


def _ret_proj(x, norm_g, w_in, layer, cos, sin):
    vec_spec = pl.BlockSpec((1, D_MODEL), lambda i, j: (0, 0))
    rope_spec = pl.BlockSpec((ROW_TILE, RET_DK // 2), lambda i, j: (i, 0))
    return pl.pallas_call(
        _proj_kernel,
        grid=(N_ROWS // ROW_TILE, RET_IN // PROJ_TILE),
        in_specs=[
            pl.BlockSpec((ROW_TILE, D_MODEL), lambda i, j: (i, 0)),
            vec_spec,
            pl.BlockSpec((1, D_MODEL, PROJ_TILE), lambda i, j: (layer, 0, j)),
            rope_spec,
            rope_spec,
        ],
        out_specs=pl.BlockSpec((ROW_TILE, PROJ_TILE), lambda i, j: (i, j)),
        out_shape=jax.ShapeDtypeStruct((N_ROWS, RET_IN), BF16),
        scratch_shapes=[pltpu.VMEM((ROW_TILE, D_MODEL), BF16)],
        compiler_params=_params("arbitrary", "arbitrary"),
        name="ret_proj",
    )(x, norm_g.reshape(1, D_MODEL), w_in, cos, sin)


def _head_norm_gate(o, g):
    mu = jnp.mean(o, axis=-1, keepdims=True)
    oc = o - mu
    var = jnp.mean(oc * oc, axis=-1, keepdims=True)
    return oc * lax.rsqrt(var + EPS) * (g * jax.nn.sigmoid(g))


def _chunk_heads(heads, lg_ref, q_ref, k_ref, v_ref, g_ref, s_scr, y_ref, chunk, valid):
    row = lax.broadcasted_iota(jnp.int32, (chunk, 1), 0).astype(F32)
    col = lax.broadcasted_iota(jnp.int32, (1, chunk), 1).astype(F32)
    diff = row - col
    for h in heads:
        lg = lg_ref[h]
        qh = q_ref[:, h * RET_DK:(h + 1) * RET_DK]
        kh = k_ref[:, h * RET_DK:(h + 1) * RET_DK]
        vh = v_ref[:, h * RET_DV:(h + 1) * RET_DV]
        decay = jnp.where(diff >= 0, jnp.exp(jnp.maximum(diff, 0.0) * lg), 0.0)
        scores = lax.dot_general(qh, kh, (((1,), (1,)), ((), ())), preferred_element_type=F32) * decay
        inner = jnp.dot(scores.astype(BF16), vh, preferred_element_type=F32)
        st = s_scr[h]
        cross = jnp.dot(qh, st.astype(BF16), preferred_element_type=F32) * jnp.exp((row + 1.0) * lg)
        o = inner + cross
        kd = kh.astype(F32) * jnp.exp((valid - 1.0 - row) * lg)
        if valid < chunk:
            kd = jnp.where(row < valid, kd, 0.0)
        kd = kd.astype(BF16)
        s_dec = jnp.exp(jnp.full((1, RET_DV), valid, F32) * lg)
        s_scr[h] = st * s_dec + lax.dot_general(kd, vh, (((0,), (0,)), ((), ())),
                                               preferred_element_type=F32)
        gh = g_ref[:, h * RET_DV:(h + 1) * RET_DV].astype(F32)
        y_ref[:, h * RET_DV:(h + 1) * RET_DV] = _head_norm_gate(o, gh).astype(BF16)


def _sample_head(h, nb, lg_ref, qt_ref, kt_ref, v_ref, g_ref, s_ref, so_ref, ysm_scr):
    lg = lg_ref[h]
    gamma = jnp.exp(jnp.full((1, 128), 1.0, F32) * lg)
    sample = lax.broadcasted_iota(jnp.int32, (DEC_BATCH, 128), 0)
    row8 = lax.broadcasted_iota(jnp.int32, (SAMPLE_BLOCK, 1), 0)
    qt = qt_ref[0]
    kt = kt_ref[0]
    o_all = jnp.zeros((SAMPLE_BLOCK, RET_DV), F32)
    for n in range(SAMPLE_BLOCK):
        onehot = (sample == nb * SAMPLE_BLOCK + n).astype(BF16)
        qb = jnp.dot(qt, onehot, preferred_element_type=F32)
        kb = jnp.dot(kt, onehot, preferred_element_type=F32)
        v_row = v_ref[n:n + 1, :]
        parts = []
        for vt in range(RET_DV // 128):
            sl = slice(vt * 128, (vt + 1) * 128)
            s_new = s_ref[0, n, 0, :, sl] * gamma + kb * v_row[:, sl]
            so_ref[0, n, 0, :, sl] = s_new
            parts.append(jnp.sum(s_new * qb, axis=0, keepdims=True))
        o_n = jnp.concatenate(parts, axis=1)
        o_all = jnp.where(row8 == n, o_n, o_all)
    rows = pl.ds(pl.multiple_of(N_META + nb * SAMPLE_BLOCK, SAMPLE_BLOCK), SAMPLE_BLOCK)
    ysm_scr[h, rows, :] = _head_norm_gate(o_all, g_ref[...])


def _ret_kernel(*refs, chunk, valid, n_chunks, n_seq, fused, aliased):
    if not fused:
        lg_ref, q_ref, k_ref, v_ref, g_ref, s0_ref, y_ref, sout_ref, s_scr = refs
        s_scr[...] = s0_ref[...]
        _chunk_heads(range(RET_HEADS), lg_ref, q_ref, k_ref, v_ref, g_ref, s_scr, y_ref, chunk, valid)
        sout_ref[0] = s_scr[...]
        return

    (lg_ref, q_ref, k_ref, v_ref, g_ref, s0_ref, ymeta_ref, qt_ref, kt_ref, vs_ref, gs_ref,
     st_ref) = refs[:12]
    y_ref, sout_ref, sto_ref, s_scr, ysm_scr = refs[12 + (1 if aliased else 0):]
    s = pl.program_id(0)
    n_main = n_seq * n_chunks * CHUNK_SPLIT
    heads_per_step = RET_HEADS // CHUNK_SPLIT

    @pl.when(s < n_main)
    def _():
        part = lax.rem(s, CHUNK_SPLIT)
        c = lax.rem(s // CHUNK_SPLIT, n_chunks)

        @pl.when(s == 0)
        def _():
            ysm_scr[...] = jnp.zeros(ysm_scr.shape, F32)
            for h in range(RET_HEADS):
                ysm_scr[h, 0:N_META, :] = ymeta_ref[0:N_META, h * RET_DV:(h + 1) * RET_DV].astype(F32)

        @pl.when(jnp.logical_and(c == 0, part == 0))
        def _():
            s_scr[...] = s0_ref[...]

        for p in range(CHUNK_SPLIT):
            @pl.when(part == p)
            def _(p=p):
                _chunk_heads(range(p * heads_per_step, (p + 1) * heads_per_step),
                             lg_ref, q_ref, k_ref, v_ref, g_ref, s_scr, y_ref, chunk, valid)

        @pl.when(jnp.logical_and(c == n_chunks - 1, part == CHUNK_SPLIT - 1))
        def _():
            sout_ref[0] = s_scr[...]

        n_blocks = DEC_BATCH // SAMPLE_BLOCK
        _sample_head(s // n_blocks, lax.rem(s, n_blocks), lg_ref, qt_ref, kt_ref, vs_ref, gs_ref,
                     st_ref, sto_ref, ysm_scr)

    @pl.when(s >= n_main)
    def _():
        rows = pl.ds(pl.multiple_of((s - n_main) * chunk, chunk), chunk)
        for h in range(RET_HEADS):
            y_ref[:, h * RET_DV:(h + 1) * RET_DV] = ysm_scr[h, rows, :].astype(BF16)


def _retention(log_gamma, qkvg, s0, sample, *, n_seq, n_chunks, chunk, valid):
    k_col = 1
    v_col = 2 * D_MODEL // RET_VDIM
    fused = sample is not None
    if not fused:
        n_steps, n_rows = n_seq * n_chunks, n_seq * n_chunks * chunk
        blk = lambda s: s
        y_blk = lambda s: s
        seq = lambda s: s // n_chunks
    else:
        y_meta, qt, kt, vs, gs, state_ret, prev_out, layer = sample
        n_small = SMALL_ROWS // chunk
        n_main = n_seq * n_chunks * CHUNK_SPLIT
        assert n_main == RET_HEADS * (DEC_BATCH // SAMPLE_BLOCK)
        n_steps, n_rows = n_main + n_small, N_ROWS
        blk = lambda s: n_small + jnp.minimum(s, n_main - 1) // CHUNK_SPLIT
        y_blk = lambda s: jnp.where(s < n_main, n_small + s // CHUNK_SPLIT, s - n_main)
        seq = lambda s: jnp.minimum(s, n_main - 1) // (CHUNK_SPLIT * n_chunks)
    in_specs = [
        pl.BlockSpec(memory_space=pltpu.SMEM),
        pl.BlockSpec((chunk, D_MODEL), lambda s: (blk(s), 0)),
        pl.BlockSpec((chunk, D_MODEL), lambda s: (blk(s), k_col)),
        pl.BlockSpec((chunk, RET_VDIM), lambda s: (blk(s), v_col)),
        pl.BlockSpec((chunk, RET_VDIM), lambda s: (blk(s), v_col + 1)),
        pl.BlockSpec((RET_HEADS, RET_DK, RET_DV), lambda s: (0, 0, 0)),
    ]
    args = [log_gamma, qkvg, qkvg, qkvg, qkvg, s0]
    out_specs = [
        pl.BlockSpec((chunk, RET_VDIM), lambda s: (y_blk(s), 0)),
        pl.BlockSpec((1, RET_HEADS, RET_DK, RET_DV), lambda s: (seq(s), 0, 0, 0)),
    ]
    out_shape = [
        jax.ShapeDtypeStruct((n_rows, RET_VDIM), BF16),
        jax.ShapeDtypeStruct((n_seq, RET_HEADS, RET_DK, RET_DV), F32),
    ]
    scratch = [pltpu.VMEM((RET_HEADS, RET_DK, RET_DV), F32)]
    aliases = {}
    if fused:
        n_blocks = DEC_BATCH // SAMPLE_BLOCK
        head = lambda s: jnp.minimum(s, n_main - 1) // n_blocks
        sblk = lambda s: lax.rem(jnp.minimum(s, n_main - 1), n_blocks)
        state_spec = pl.BlockSpec((1, SAMPLE_BLOCK, 1, RET_DK, RET_DV),
                                  lambda s: (layer, sblk(s), head(s), 0, 0))
        t_spec = pl.BlockSpec((1, RET_DK, DEC_BATCH), lambda s: (head(s), 0, 0))
        vg_spec = pl.BlockSpec((SAMPLE_BLOCK, RET_DV), lambda s: (sblk(s), head(s)))
        in_specs += [pl.BlockSpec((chunk, RET_VDIM), lambda s: (0, 0)), t_spec, t_spec, vg_spec, vg_spec,
                     state_spec]
        args += [y_meta, qt, kt, vs, gs, state_ret]
        if prev_out is not None:
            aliases = {len(args): 2}
            in_specs.append(pl.BlockSpec(memory_space=pl.ANY))
            args.append(prev_out)
        out_specs.append(state_spec)
        out_shape.append(jax.ShapeDtypeStruct(state_ret.shape, F32))
        scratch.append(pltpu.VMEM((RET_HEADS, SMALL_ROWS, RET_DV), F32))
    return pl.pallas_call(
        functools.partial(_ret_kernel, chunk=chunk, valid=valid, n_chunks=n_chunks, n_seq=n_seq,
                          fused=fused, aliased=bool(aliases)),
        grid=(n_steps,),
        in_specs=in_specs,
        out_specs=out_specs,
        out_shape=out_shape,
        scratch_shapes=scratch,
        input_output_aliases=aliases,
        compiler_params=_params("arbitrary"),
        name="retention",
    )(*args)


def _outproj_kernel(x_ref, y_ref, w_ref, o_ref):
    o_ref[...] = x_ref[...] + jnp.dot(y_ref[...], w_ref[0].astype(BF16), preferred_element_type=F32)


def _ret_outproj(x, y, w_out, layer):
    x_spec = pl.BlockSpec((ROW_TILE, OUT_TILE), lambda i, n: (i, n))
    return pl.pallas_call(
        _outproj_kernel,
        grid=(N_ROWS // ROW_TILE, D_MODEL // OUT_TILE),
        in_specs=[
            x_spec,
            pl.BlockSpec((ROW_TILE, RET_VDIM), lambda i, n: (i, 0)),
            pl.BlockSpec((1, RET_VDIM, OUT_TILE), lambda i, n: (layer, 0, n)),
        ],
        out_specs=x_spec,
        out_shape=jax.ShapeDtypeStruct((N_ROWS, D_MODEL), F32),
        compiler_params=_params("arbitrary", "arbitrary"),
        name="ret_outproj",
    )(x, y, w_out)


def _rope_tables():
    pos = np.zeros((N_ROWS,), np.int32)
    pos[:N_META] = np.arange(N_META)
    pos[N_META:N_META + DEC_BATCH] = PAST_LEN
    pos[SMALL_ROWS:] = np.tile(N_META + np.arange(SEQ), BATCH)
    half = RET_DK // 2
    inv = ROPE_BASE ** (-jnp.arange(half, dtype=F32) / half)
    ang = jnp.asarray(pos).astype(F32)[:, None] * inv[None, :]
    return jnp.cos(ang), jnp.sin(ang)


def kernel(x_prompt, x_sample, state_pool, state_ret, meta_tokens, pool_norm, pool_w, pool_scale,
           ret_norm, ret_w_in, ret_w_out, mlp_norm, mlp_w_up, mlp_w_down, final_norm):
    assert x_prompt.shape == (BATCH, SEQ, D_MODEL) and x_sample.shape == (DEC_BATCH, 1, D_MODEL)
    pad_rows = SMALL_ROWS - N_META - DEC_BATCH
    x_small = jnp.concatenate([
        meta_tokens.astype(F32),
        x_sample.reshape(DEC_BATCH, D_MODEL),
        jnp.zeros((pad_rows, D_MODEL), F32),
    ], axis=0)
    x = None
    log_gamma = jnp.log1p(-jnp.exp2(-5.0 - jnp.arange(RET_HEADS, dtype=F32)))
    cos, sin = _rope_tables()
    state_slots = state_pool.transpose(0, 2, 1, 3)
    carry_sums = _pool_carry_sums(state_slots)

    pool_p, pool_s, ret_p = [], [], []
    ret_s = None
    for i in range(DEPTH):
        j = i // 2
        if i % 2 == 0:
            carry_ext = jnp.pad(carry_sums[j], ((N_META, pad_rows), (0, 0)))
            if x is None:
                xs, xr, block0 = x_small, x_prompt.reshape(BATCH * SEQ, D_MODEL), 0
            else:
                xs, xr, block0 = x, x, SMALL_ROWS // POOL_TILE
            x, u_small, u_tail = _pool_layer(xs, xr, block0, carry_ext, pool_norm[j], pool_scale[j], pool_w, j)
            pool_p.append(u_tail[:, 1:])
            pool_s.append(u_small)
        else:
            qkvg = _ret_proj(x, ret_norm[j], ret_w_in, j, cos, sin)
            zero_state = jnp.zeros((RET_HEADS, RET_DK, RET_DV), F32)
            y_meta, s_meta = _retention(log_gamma, qkvg, zero_state, None,
                                        n_seq=1, n_chunks=1, chunk=RET_CHUNK, valid=N_META)
            qkvg_s = qkvg[N_META:N_META + DEC_BATCH]
            qt = qkvg_s[:, :D_MODEL].reshape(DEC_BATCH, RET_HEADS, RET_DK).transpose(1, 2, 0)
            kt = qkvg_s[:, D_MODEL:2 * D_MODEL].reshape(DEC_BATCH, RET_HEADS, RET_DK).transpose(1, 2, 0)
            sample = (y_meta, qt, kt, qkvg_s[:, 2 * D_MODEL:2 * D_MODEL + RET_VDIM].astype(F32),
                      qkvg_s[:, 2 * D_MODEL + RET_VDIM:].astype(F32), state_ret, ret_s, j)
            y, s_real, ret_s = _retention(log_gamma, qkvg, s_meta[0], sample, n_seq=BATCH,
                                          n_chunks=SEQ // RET_CHUNK, chunk=RET_CHUNK, valid=RET_CHUNK)
            ret_p.append(s_real)
            x = _ret_outproj(x, y, ret_w_out, j)
        x = _mlp_layer(x, mlp_norm[i], mlp_w_up, mlp_w_down, i, final_norm, i == DEPTH - 1)

    y_prompt = x[SMALL_ROWS:].reshape(BATCH, SEQ, D_MODEL)
    y_sample = x[N_META:N_META + DEC_BATCH].reshape(DEC_BATCH, 1, D_MODEL)
    new_pool_sample = _pool_sample_states(state_slots, *pool_s).transpose(0, 2, 1, 3)
    return (y_prompt, y_sample, jnp.stack(pool_p, axis=0), new_pool_sample,
            jnp.stack(ret_p, axis=0), ret_s)
```

```python
import functools

import numpy as np
import jax
import jax.numpy as jnp
from jax import lax
from jax.experimental import pallas as pl
from jax.experimental.pallas import tpu as pltpu

D_MODEL = 2048
BATCH = 4
SEQ = 2048
DEPTH = 4
DEC_BATCH = 128
PAST_LEN = 16384
N_META = 16
POOL_WINDOWS = (2, 4, 8, 16)
POOL_GROUP_DIM = D_MODEL // len(POOL_WINDOWS)
POOL_BUF = max(POOL_WINDOWS) - 1
RET_HEADS = 8
RET_DK = D_MODEL // RET_HEADS
RET_DV = 2 * RET_DK
RET_VDIM = RET_HEADS * RET_DV
RET_IN = 2 * D_MODEL + 2 * RET_VDIM
RET_CHUNK = 128
ROPE_BASE = 10000.0
D_FF = 4 * D_MODEL
EPS = 1e-6

SMALL_ROWS = 256
N_ROWS = SMALL_ROWS + BATCH * SEQ
ROW_TILE = N_ROWS // 8
POOL_TILE = 256
POOL_TILES_PER_BATCH = SEQ // POOL_TILE
FF_TILE = 1024
FF_CHUNK = 512
PROJ_TILE = 2048
OUT_TILE = 512
SAMPLE_BLOCK = 8
CHUNK_SPLIT = 2
VMEM_CAPACITY = 64 * 1024 * 1024
VMEM_LIMIT = 60 * 1024 * 1024

F32 = jnp.float32
BF16 = jnp.bfloat16


def _rmsnorm(x, g):
    return x * lax.rsqrt(jnp.mean(x * x, axis=-1, keepdims=True) + EPS) * g


def _params(*semantics):
    return pltpu.CompilerParams(dimension_semantics=semantics, vmem_limit_bytes=VMEM_LIMIT)


def _poolsum_kernel(s_ref, o_ref):
    for gi, w in enumerate(POOL_WINDOWS):
        sl = slice(gi * POOL_GROUP_DIM, (gi + 1) * POOL_GROUP_DIM)
        acc = s_ref[0, POOL_BUF - 1, :, sl]
        for r in range(POOL_BUF - (w - 1), POOL_BUF - 1):
            acc = acc + s_ref[0, r, :, sl]
        o_ref[0, :, sl] = acc


def _pool_carry_sums(state_slots):
    n_layers = state_slots.shape[0]
    n_rows = 32
    return pl.pallas_call(
        _poolsum_kernel,
        grid=(n_layers, DEC_BATCH // n_rows),
        in_specs=[pl.BlockSpec((1, POOL_BUF, n_rows, D_MODEL), lambda j, n: (j, 0, n, 0))],
        out_specs=pl.BlockSpec((1, n_rows, D_MODEL), lambda j, n: (j, n, 0)),
        out_shape=jax.ShapeDtypeStruct((n_layers, DEC_BATCH, D_MODEL), F32),
        compiler_params=_params("arbitrary", "arbitrary"),
        name="pool_carry_sums",
    )(state_slots)


def _pool_state_kernel(s_ref, u0_ref, u1_ref, o_ref):
    r = pl.program_id(1)

    @pl.when(r < POOL_BUF - 1)
    def _():
        o_ref[0, 0] = s_ref[0, 0]

    @pl.when(r == POOL_BUF - 1)
    def _():
        u = jnp.where(pl.program_id(0) == 0, u0_ref[...], u1_ref[...])
        o_ref[0, 0] = u[N_META:N_META + DEC_BATCH, :]


def _pool_sample_states(state_slots, u_small0, u_small1):
    u_spec = pl.BlockSpec((SMALL_ROWS, D_MODEL), lambda j, r: (0, 0))
    return pl.pallas_call(
        _pool_state_kernel,
        grid=(state_slots.shape[0], POOL_BUF),
        in_specs=[
            pl.BlockSpec((1, 1, DEC_BATCH, D_MODEL), lambda j, r: (j, jnp.minimum(r + 1, POOL_BUF - 1), 0, 0)),
            u_spec,
            u_spec,
        ],
        out_specs=pl.BlockSpec((1, 1, DEC_BATCH, D_MODEL), lambda j, r: (j, r, 0, 0)),
        out_shape=jax.ShapeDtypeStruct(state_slots.shape, F32),
        compiler_params=_params("arbitrary", "arbitrary"),
        name="pool_sample_states",
    )(state_slots, u_small0, u_small1)


def _pool_kernel(xs_ref, xr_ref, carry_ref, g_ref, sc_ref, w_ref, o_ref, usmall_ref, utail_ref,
                 prev_scr, meta_scr, w_scr):
    t = pl.program_id(0)
    is_small = t == 0
    chunk = lax.rem(jnp.maximum(t - 1, 0), POOL_TILES_PER_BATCH)
    x = jnp.where(is_small, xs_ref[...], xr_ref[...])
    u = _rmsnorm(x, g_ref[...])

    @pl.when(is_small)
    def _():
        prev_scr[...] = jnp.zeros((16, D_MODEL), F32)
        meta_scr[...] = u[0:16, :]
        usmall_ref[...] = u
        w_scr[...] = w_ref[0].astype(BF16)

    @pl.when(jnp.logical_and(t > 0, chunk == 0))
    def _():
        prev_scr[...] = meta_scr[...]

    ext = jnp.concatenate([prev_scr[...], u], axis=0)
    row = lax.broadcasted_iota(jnp.int32, (POOL_TILE, 1), 0)
    meta_row = jnp.logical_and(is_small, row < N_META)
    sample_row = jnp.logical_and(is_small, row >= N_META)
    for gi, w in enumerate(POOL_WINDOWS):
        sl = slice(gi * POOL_GROUP_DIM, (gi + 1) * POOL_GROUP_DIM)
        s = ext[:, sl]
        shift = 1
        while shift < w:
            s = s + pltpu.roll(s, shift, 0)
            shift *= 2
        ug = u[:, sl]
        ws = jnp.where(sample_row, carry_ref[:, sl] + ug, s[16:, :])
        inv_cnt = 1.0 / jnp.where(meta_row, jnp.minimum(w, row + 1), w).astype(F32)
        d = (ws * inv_cnt - ug).astype(BF16)
        y = jnp.dot(d, w_scr[gi], preferred_element_type=F32)
        o_ref[:, sl] = x[:, sl] + y * sc_ref[:, sl]

    tail = u[POOL_TILE - 16:, :]
    prev_scr[...] = tail

    @pl.when(jnp.logical_and(t > 0, chunk == POOL_TILES_PER_BATCH - 1))
    def _():
        utail_ref[0] = tail


def _pool_layer(x_small, x_real, real_block0, carry_ext, norm_g, scale, pool_w, layer):
    n_tiles = N_ROWS // POOL_TILE
    row_spec = pl.BlockSpec((POOL_TILE, D_MODEL), lambda t: (t, 0))
    first_spec = pl.BlockSpec((POOL_TILE, D_MODEL), lambda t: (0, 0))
    vec_spec = pl.BlockSpec((1, D_MODEL), lambda t: (0, 0))
    n_groups = len(POOL_WINDOWS)
    return pl.pallas_call(
        _pool_kernel,
        grid=(n_tiles,),
        in_specs=[
            first_spec,
            pl.BlockSpec((POOL_TILE, D_MODEL), lambda t: (jnp.maximum(t - 1, 0) + real_block0, 0)),
            first_spec,
            vec_spec,
            vec_spec,
            pl.BlockSpec((1, n_groups, POOL_GROUP_DIM, POOL_GROUP_DIM), lambda t: (layer, 0, 0, 0)),
        ],
        out_specs=[
            row_spec,
            first_spec,
            pl.BlockSpec((1, 16, D_MODEL), lambda t: (jnp.maximum(t - 1, 0) // POOL_TILES_PER_BATCH, 0, 0)),
        ],
        out_shape=[
            jax.ShapeDtypeStruct((N_ROWS, D_MODEL), F32),
            jax.ShapeDtypeStruct((SMALL_ROWS, D_MODEL), F32),
            jax.ShapeDtypeStruct((BATCH, 16, D_MODEL), F32),
        ],
        scratch_shapes=[
            pltpu.VMEM((16, D_MODEL), F32),
            pltpu.VMEM((16, D_MODEL), F32),
            pltpu.VMEM((n_groups, POOL_GROUP_DIM, POOL_GROUP_DIM), BF16),
        ],
        compiler_params=_params("arbitrary"),
        name="pool_layer",
    )(x_small, x_real, carry_ext, norm_g.reshape(1, D_MODEL), scale.reshape(1, D_MODEL), pool_w)


def _consume_row_tile(x_hbm, x_scr, x_sem, consume):
    i = pl.program_id(0)
    j = pl.program_id(1)

    def x_copy(tile):
        rows = pl.ds(pl.multiple_of(tile * ROW_TILE, ROW_TILE), ROW_TILE)
        return pltpu.make_async_copy(x_hbm.at[rows, :], x_scr, x_sem)

    @pl.when(jnp.logical_and(i == 0, j == 0))
    def _():
        x_copy(0).start()

    @pl.when(j == 0)
    def _():
        x_copy(i).wait()
        consume()

    @pl.when(jnp.logical_and(j == 1, i + 1 < pl.num_programs(0)))
    def _():
        x_copy(i + 1).start()


def _mlp_kernel(x_hbm, g_ref, wu_ref, wd_ref, gf_ref, o_ref, x_scr, u_scr, x_sem, *, final_norm):
    j = pl.program_id(1)

    def start_tile():
        x = x_scr[...]
        u_scr[...] = _rmsnorm(x, g_ref[...]).astype(BF16)
        o_ref[...] = x

    _consume_row_tile(x_hbm, x_scr, x_sem, start_tile)

    for c in range(FF_TILE // FF_CHUNK):
        ff = slice(c * FF_CHUNK, (c + 1) * FF_CHUNK)
        h = jnp.dot(u_scr[...], wu_ref[0, :, ff].astype(BF16), preferred_element_type=F32)
        h = jnp.square(jnp.maximum(h, 0.0)).astype(BF16)
        for n in range(D_MODEL // OUT_TILE):
            sl = slice(n * OUT_TILE, (n + 1) * OUT_TILE)
            o_ref[:, sl] += jnp.dot(h, wd_ref[0, ff, sl].astype(BF16), preferred_element_type=F32)

    if final_norm:
        @pl.when(j == pl.num_programs(1) - 1)
        def _():
            o_ref[...] = _rmsnorm(o_ref[...], gf_ref[...])


def _mlp_layer(x, norm_g, w_up, w_down, layer, final_g, final_norm):
    row_spec = pl.BlockSpec((ROW_TILE, D_MODEL), lambda i, j: (i, 0))
    vec_spec = pl.BlockSpec((1, D_MODEL), lambda i, j: (0, 0))
    return pl.pallas_call(
        functools.partial(_mlp_kernel, final_norm=final_norm),
        grid=(N_ROWS // ROW_TILE, D_FF // FF_TILE),
        in_specs=[
            pl.BlockSpec(memory_space=pl.ANY),
            vec_spec,
            pl.BlockSpec((1, D_MODEL, FF_TILE), lambda i, j: (layer, 0, j)),
            pl.BlockSpec((1, FF_TILE, D_MODEL), lambda i, j: (layer, j, 0)),
            vec_spec,
        ],
        out_specs=row_spec,
        out_shape=jax.ShapeDtypeStruct((N_ROWS, D_MODEL), F32),
        scratch_shapes=[
            pltpu.VMEM((ROW_TILE, D_MODEL), F32),
            pltpu.VMEM((ROW_TILE, D_MODEL), BF16),
            pltpu.SemaphoreType.DMA(()),
        ],
        compiler_params=pltpu.CompilerParams(dimension_semantics=("arbitrary", "arbitrary"),
                                             vmem_limit_bytes=VMEM_CAPACITY),
        name="mlp_layer",
    )(x, norm_g.reshape(1, D_MODEL), w_up, w_down, final_g.reshape(1, D_MODEL))


def _proj_kernel(x_hbm, g_ref, w_ref, cos_ref, sin_ref, o_ref, x_scr, u_scr, x_sem):
    j = pl.program_id(1)
    n_rope_tiles = 2 * D_MODEL // PROJ_TILE

    def start_tile():
        u_scr[...] = _rmsnorm(x_scr[...], g_ref[...]).astype(BF16)

    _consume_row_tile(x_hbm, x_scr, x_sem, start_tile)

    is_rope = j < n_rope_tiles
    k_scale = jnp.where(j >= n_rope_tiles // 2, RET_DK ** -0.5, 1.0).astype(F32)
    cos = jnp.where(is_rope, cos_ref[...] * k_scale, 1.0)
    sin = jnp.where(is_rope, sin_ref[...] * k_scale, 0.0)
    half = RET_DK // 2
    for hh in range(PROJ_TILE // RET_DK):
        acc = jnp.dot(u_scr[...], w_ref[0, :, hh * RET_DK:(hh + 1) * RET_DK].astype(BF16),
                      preferred_element_type=F32)
        x1 = acc[:, :half]
        x2 = acc[:, half:]
        o_ref[:, hh * RET_DK:hh * RET_DK + half] = (x1 * cos - x2 * sin).astype(BF16)
        o_ref[:, hh * RET_DK + half:(hh + 1) * RET_DK] = (x1 * sin + x2 * cos).astype(BF16)


def _ret_proj(x, norm_g, w_in, layer, cos, sin):
    vec_spec = pl.BlockSpec((1, D_MODEL), lambda i, j: (0, 0))
    rope_spec = pl.BlockSpec((ROW_TILE, RET_DK // 2), lambda i, j: (i, 0))
    return pl.pallas_call(
        _proj_kernel,
        grid=(N_ROWS // ROW_TILE, RET_IN // PROJ_TILE),
        in_specs=[
            pl.BlockSpec(memory_space=pl.ANY),
            vec_spec,
            pl.BlockSpec((1, D_MODEL, PROJ_TILE), lambda i, j: (layer, 0, j)),
            rope_spec,
            rope_spec,
        ],
        out_specs=pl.BlockSpec((ROW_TILE, PROJ_TILE), lambda i, j: (i, j)),
        out_shape=jax.ShapeDtypeStruct((N_ROWS, RET_IN), BF16),
        scratch_shapes=[
            pltpu.VMEM((ROW_TILE, D_MODEL), F32),
            pltpu.VMEM((ROW_TILE, D_MODEL), BF16),
            pltpu.SemaphoreType.DMA(()),
        ],
        compiler_params=pltpu.CompilerParams(dimension_semantics=("arbitrary", "arbitrary"),
                                             vmem_limit_bytes=VMEM_CAPACITY),
        name="ret_proj",
    )(x, norm_g.reshape(1, D_MODEL), w_in, cos, sin)


def _head_norm_gate(o, g):
    mu = jnp.mean(o, axis=-1, keepdims=True)
    oc = o - mu
    var = jnp.mean(oc * oc, axis=-1, keepdims=True)
    return oc * lax.rsqrt(var + EPS) * (g * jax.nn.sigmoid(g))


def _chunk_heads(heads, lg_ref, q_ref, k_ref, v_ref, g_ref, s_scr, y_ref, chunk, valid):
    row = lax.broadcasted_iota(jnp.int32, (chunk, 1), 0).astype(F32)
    col = lax.broadcasted_iota(jnp.int32, (1, chunk), 1).astype(F32)
    diff = row - col
    for h in heads:
        lg = lg_ref[h]
        qh = q_ref[:, h * RET_DK:(h + 1) * RET_DK]
        kh = k_ref[:, h * RET_DK:(h + 1) * RET_DK]
        vh = v_ref[:, h * RET_DV:(h + 1) * RET_DV]
        decay = jnp.where(diff >= 0, jnp.exp(jnp.maximum(diff, 0.0) * lg), 0.0)
        scores = lax.dot_general(qh, kh, (((1,), (1,)), ((), ())), preferred_element_type=F32) * decay
        inner = jnp.dot(scores.astype(BF16), vh, preferred_element_type=F32)
        st = s_scr[h]
        cross = jnp.dot(qh, st.astype(BF16), preferred_element_type=F32) * jnp.exp((row + 1.0) * lg)
        o = inner + cross
        kd = kh.astype(F32) * jnp.exp((valid - 1.0 - row) * lg)
        if valid < chunk:
            kd = jnp.where(row < valid, kd, 0.0)
        kd = kd.astype(BF16)
        s_dec = jnp.exp(jnp.full((1, RET_DV), valid, F32) * lg)
        s_scr[h] = st * s_dec + lax.dot_general(kd, vh, (((0,), (0,)), ((), ())),
                                               preferred_element_type=F32)
        gh = g_ref[:, h * RET_DV:(h + 1) * RET_DV].astype(F32)
        y_ref[:, h * RET_DV:(h + 1) * RET_DV] = _head_norm_gate(o, gh).astype(BF16)


def _sample_head(h, nb, lg_ref, qt_ref, kt_ref, v_ref, g_ref, s_ref, so_ref, ysm_scr):
    lg = lg_ref[h]
    gamma = jnp.exp(jnp.full((1, 128), 1.0, F32) * lg)
    sample = lax.broadcasted_iota(jnp.int32, (DEC_BATCH, 128), 0)
    row8 = lax.broadcasted_iota(jnp.int32, (SAMPLE_BLOCK, 1), 0)
    qt = qt_ref[0]
    kt = kt_ref[0]
    o_all = jnp.zeros((SAMPLE_BLOCK, RET_DV), F32)
    for n in range(SAMPLE_BLOCK):
        onehot = (sample == nb * SAMPLE_BLOCK + n).astype(BF16)
        qb = jnp.dot(qt, onehot, preferred_element_type=F32)
        kb = jnp.dot(kt, onehot, preferred_element_type=F32)
        v_row = v_ref[n:n + 1, :]
        parts = []
        for vt in range(RET_DV // 128):
            sl = slice(vt * 128, (vt + 1) * 128)
            s_new = s_ref[0, n, 0, :, sl] * gamma + kb * v_row[:, sl]
            so_ref[0, n, 0, :, sl] = s_new
            parts.append(jnp.sum(s_new * qb, axis=0, keepdims=True))
        o_n = jnp.concatenate(parts, axis=1)
        o_all = jnp.where(row8 == n, o_n, o_all)
    rows = pl.ds(pl.multiple_of(N_META + nb * SAMPLE_BLOCK, SAMPLE_BLOCK), SAMPLE_BLOCK)
    ysm_scr[h, rows, :] = _head_norm_gate(o_all, g_ref[...])


def _ret_kernel(*refs, chunk, valid, n_chunks, n_seq, fused, aliased):
    if not fused:
        lg_ref, q_ref, k_ref, v_ref, g_ref, s0_ref, y_ref, sout_ref, s_scr = refs
        s_scr[...] = s0_ref[...]
        _chunk_heads(range(RET_HEADS), lg_ref, q_ref, k_ref, v_ref, g_ref, s_scr, y_ref, chunk, valid)
        sout_ref[0] = s_scr[...]
        return

    (lg_ref, q_ref, k_ref, v_ref, g_ref, s0_ref, ymeta_ref, qt_ref, kt_ref, vs_ref, gs_ref,
     st_ref) = refs[:12]
    y_ref, sout_ref, sto_ref, s_scr, ysm_scr = refs[12 + (1 if aliased else 0):]
    s = pl.program_id(0)
    n_main = n_seq * n_chunks * CHUNK_SPLIT
    heads_per_step = RET_HEADS // CHUNK_SPLIT

    @pl.when(s < n_main)
    def _():
        part = lax.rem(s, CHUNK_SPLIT)
        c = lax.rem(s // CHUNK_SPLIT, n_chunks)

        @pl.when(s == 0)
        def _():
            ysm_scr[...] = jnp.zeros(ysm_scr.shape, F32)
            for h in range(RET_HEADS):
                ysm_scr[h, 0:N_META, :] = ymeta_ref[0:N_META, h * RET_DV:(h + 1) * RET_DV].astype(F32)

        @pl.when(jnp.logical_and(c == 0, part == 0))
        def _():
            s_scr[...] = s0_ref[...]

        for p in range(CHUNK_SPLIT):
            @pl.when(part == p)
            def _(p=p):
                _chunk_heads(range(p * heads_per_step, (p + 1) * heads_per_step),
                             lg_ref, q_ref, k_ref, v_ref, g_ref, s_scr, y_ref, chunk, valid)

        @pl.when(jnp.logical_and(c == n_chunks - 1, part == CHUNK_SPLIT - 1))
        def _():
            sout_ref[0] = s_scr[...]

        n_blocks = DEC_BATCH // SAMPLE_BLOCK
        _sample_head(s // n_blocks, lax.rem(s, n_blocks), lg_ref, qt_ref, kt_ref, vs_ref, gs_ref,
                     st_ref, sto_ref, ysm_scr)

    @pl.when(s >= n_main)
    def _():
        rows = pl.ds(pl.multiple_of((s - n_main) * chunk, chunk), chunk)
        for h in range(RET_HEADS):
            y_ref[:, h * RET_DV:(h + 1) * RET_DV] = ysm_scr[h, rows, :].astype(BF16)


def _retention(log_gamma, qkvg, s0, sample, *, n_seq, n_chunks, chunk, valid):
    k_col = 1
    v_col = 2 * D_MODEL // RET_VDIM
    fused = sample is not None
    if not fused:
        n_steps, n_rows = n_seq * n_chunks, n_seq * n_chunks * chunk
        blk = lambda s: s
        y_blk = lambda s: s
        seq = lambda s: s // n_chunks
    else:
        y_meta, qt, kt, vs, gs, state_ret, prev_out, layer = sample
        n_small = SMALL_ROWS // chunk
        n_main = n_seq * n_chunks * CHUNK_SPLIT
        assert n_main == RET_HEADS * (DEC_BATCH // SAMPLE_BLOCK)
        n_steps, n_rows = n_main + n_small, N_ROWS
        blk = lambda s: n_small + jnp.minimum(s, n_main - 1) // CHUNK_SPLIT
        y_blk = lambda s: jnp.where(s < n_main, n_small + s // CHUNK_SPLIT, s - n_main)
        seq = lambda s: jnp.minimum(s, n_main - 1) // (CHUNK_SPLIT * n_chunks)
    in_specs = [
        pl.BlockSpec(memory_space=pltpu.SMEM),
        pl.BlockSpec((chunk, D_MODEL), lambda s: (blk(s), 0)),
        pl.BlockSpec((chunk, D_MODEL), lambda s: (blk(s), k_col)),
        pl.BlockSpec((chunk, RET_VDIM), lambda s: (blk(s), v_col)),
        pl.BlockSpec((chunk, RET_VDIM), lambda s: (blk(s), v_col + 1)),
        pl.BlockSpec((RET_HEADS, RET_DK, RET_DV), lambda s: (0, 0, 0)),
    ]
    args = [log_gamma, qkvg, qkvg, qkvg, qkvg, s0]
    out_specs = [
        pl.BlockSpec((chunk, RET_VDIM), lambda s: (y_blk(s), 0)),
        pl.BlockSpec((1, RET_HEADS, RET_DK, RET_DV), lambda s: (seq(s), 0, 0, 0)),
    ]
    out_shape = [
        jax.ShapeDtypeStruct((n_rows, RET_VDIM), BF16),
        jax.ShapeDtypeStruct((n_seq, RET_HEADS, RET_DK, RET_DV), F32),
    ]
    scratch = [pltpu.VMEM((RET_HEADS, RET_DK, RET_DV), F32)]
    aliases = {}
    if fused:
        n_blocks = DEC_BATCH // SAMPLE_BLOCK
        head = lambda s: jnp.minimum(s, n_main - 1) // n_blocks
        sblk = lambda s: lax.rem(jnp.minimum(s, n_main - 1), n_blocks)
        state_spec = pl.BlockSpec((1, SAMPLE_BLOCK, 1, RET_DK, RET_DV),
                                  lambda s: (layer, sblk(s), head(s), 0, 0))
        t_spec = pl.BlockSpec((1, RET_DK, DEC_BATCH), lambda s: (head(s), 0, 0))
        vg_spec = pl.BlockSpec((SAMPLE_BLOCK, RET_DV), lambda s: (sblk(s), head(s)))
        in_specs += [pl.BlockSpec((chunk, RET_VDIM), lambda s: (0, 0)), t_spec, t_spec, vg_spec, vg_spec,
                     state_spec]
        args += [y_meta, qt, kt, vs, gs, state_ret]
        if prev_out is not None:
            aliases = {len(args): 2}
            in_specs.append(pl.BlockSpec(memory_space=pl.ANY))
            args.append(prev_out)
        out_specs.append(state_spec)
        out_shape.append(jax.ShapeDtypeStruct(state_ret.shape, F32))
        scratch.append(pltpu.VMEM((RET_HEADS, SMALL_ROWS, RET_DV), F32))
    return pl.pallas_call(
        functools.partial(_ret_kernel, chunk=chunk, valid=valid, n_chunks=n_chunks, n_seq=n_seq,
                          fused=fused, aliased=bool(aliases)),
        grid=(n_steps,),
        in_specs=in_specs,
        out_specs=out_specs,
        out_shape=out_shape,
        scratch_shapes=scratch,
        input_output_aliases=aliases,
        compiler_params=_params("arbitrary"),
        name="retention",
    )(*args)


def _outproj_kernel(x_ref, y_ref, w_ref, o_ref):
    o_ref[...] = x_ref[...] + jnp.dot(y_ref[...], w_ref[0].astype(BF16), preferred_element_type=F32)


def _ret_outproj(x, y, w_out, layer):
    x_spec = pl.BlockSpec((ROW_TILE, OUT_TILE), lambda i, n: (i, n))
    return pl.pallas_call(
        _outproj_kernel,
        grid=(N_ROWS // ROW_TILE, D_MODEL // OUT_TILE),
        in_specs=[
            x_spec,
            pl.BlockSpec((ROW_TILE, RET_VDIM), lambda i, n: (i, 0)),
            pl.BlockSpec((1, RET_VDIM, OUT_TILE), lambda i, n: (layer, 0, n)),
        ],
        out_specs=x_spec,
        out_shape=jax.ShapeDtypeStruct((N_ROWS, D_MODEL), F32),
        compiler_params=_params("arbitrary", "arbitrary"),
        name="ret_outproj",
    )(x, y, w_out)


def _rope_tables():
    pos = np.zeros((N_ROWS,), np.int32)
    pos[:N_META] = np.arange(N_META)
    pos[N_META:N_META + DEC_BATCH] = PAST_LEN
    pos[SMALL_ROWS:] = np.tile(N_META + np.arange(SEQ), BATCH)
    half = RET_DK // 2
    inv = ROPE_BASE ** (-jnp.arange(half, dtype=F32) / half)
    ang = jnp.asarray(pos).astype(F32)[:, None] * inv[None, :]
    return jnp.cos(ang), jnp.sin(ang)


def kernel(x_prompt, x_sample, state_pool, state_ret, meta_tokens, pool_norm, pool_w, pool_scale,
           ret_norm, ret_w_in, ret_w_out, mlp_norm, mlp_w_up, mlp_w_down, final_norm):
    assert x_prompt.shape == (BATCH, SEQ, D_MODEL) and x_sample.shape == (DEC_BATCH, 1, D_MODEL)
    pad_rows = SMALL_ROWS - N_META - DEC_BATCH
    x_small = jnp.concatenate([
        meta_tokens.astype(F32),
        x_sample.reshape(DEC_BATCH, D_MODEL),
        jnp.zeros((pad_rows, D_MODEL), F32),
    ], axis=0)
    x = None
    log_gamma = jnp.log1p(-jnp.exp2(-5.0 - jnp.arange(RET_HEADS, dtype=F32)))
    cos, sin = _rope_tables()
    state_slots = state_pool.transpose(0, 2, 1, 3)
    carry_sums = _pool_carry_sums(state_slots)

    pool_p, pool_s, ret_p = [], [], []
    ret_s = None
    for i in range(DEPTH):
        j = i // 2
        if i % 2 == 0:
            carry_ext = jnp.pad(carry_sums[j], ((N_META, pad_rows), (0, 0)))
            if x is None:
                xs, xr, block0 = x_small, x_prompt.reshape(BATCH * SEQ, D_MODEL), 0
            else:
                xs, xr, block0 = x, x, SMALL_ROWS // POOL_TILE
            x, u_small, u_tail = _pool_layer(xs, xr, block0, carry_ext, pool_norm[j], pool_scale[j], pool_w, j)
            pool_p.append(u_tail[:, 1:])
            pool_s.append(u_small)
        else:
            qkvg = _ret_proj(x, ret_norm[j], ret_w_in, j, cos, sin)
            zero_state = jnp.zeros((RET_HEADS, RET_DK, RET_DV), F32)
            y_meta, s_meta = _retention(log_gamma, qkvg, zero_state, None,
                                        n_seq=1, n_chunks=1, chunk=RET_CHUNK, valid=N_META)
            qkvg_s = qkvg[N_META:N_META + DEC_BATCH]
            qt = qkvg_s[:, :D_MODEL].reshape(DEC_BATCH, RET_HEADS, RET_DK).transpose(1, 2, 0)
            kt = qkvg_s[:, D_MODEL:2 * D_MODEL].reshape(DEC_BATCH, RET_HEADS, RET_DK).transpose(1, 2, 0)
            sample = (y_meta, qt, kt, qkvg_s[:, 2 * D_MODEL:2 * D_MODEL + RET_VDIM].astype(F32),
                      qkvg_s[:, 2 * D_MODEL + RET_VDIM:].astype(F32), state_ret, ret_s, j)
            y, s_real, ret_s = _retention(log_gamma, qkvg, s_meta[0], sample, n_seq=BATCH,
                                          n_chunks=SEQ // RET_CHUNK, chunk=RET_CHUNK, valid=RET_CHUNK)
            ret_p.append(s_real)
            x = _ret_outproj(x, y, ret_w_out, j)
        x = _mlp_layer(x, mlp_norm[i], mlp_w_up, mlp_w_down, i, final_norm, i == DEPTH - 1)

    y_prompt = x[SMALL_ROWS:].reshape(BATCH, SEQ, D_MODEL)
    y_sample = x[N_META:N_META + DEC_BATCH].reshape(DEC_BATCH, 1, D_MODEL)
    new_pool_sample = _pool_sample_states(state_slots, *pool_s).transpose(0, 2, 1, 3)
    return (y_prompt, y_sample, jnp.stack(pool_p, axis=0), new_pool_sample,
            jnp.stack(ret_p, axis=0), ret_s)
```

```python
import functools

import numpy as np
import jax
import jax.numpy as jnp
from jax import lax
from jax.experimental import pallas as pl
from jax.experimental.pallas import tpu as pltpu

D_MODEL = 2048
BATCH = 4
SEQ = 2048
DEPTH = 4
DEC_BATCH = 128
PAST_LEN = 16384
N_META = 16
POOL_WINDOWS = (2, 4, 8, 16)
POOL_GROUP_DIM = D_MODEL // len(POOL_WINDOWS)
POOL_BUF = max(POOL_WINDOWS) - 1
RET_HEADS = 8
RET_DK = D_MODEL // RET_HEADS
RET_DV = 2 * RET_DK
RET_VDIM = RET_HEADS * RET_DV
RET_IN = 2 * D_MODEL + 2 * RET_VDIM
RET_CHUNK = 128
ROPE_BASE = 10000.0
D_FF = 4 * D_MODEL
EPS = 1e-6

SMALL_ROWS = 256
N_ROWS = SMALL_ROWS + BATCH * SEQ
ROW_TILE = N_ROWS // 8
POOL_TILE = 256
POOL_TILES_PER_BATCH = SEQ // POOL_TILE
FF_TILE = 1024
FF_CHUNK = 512
PROJ_TILE = 2048
OUT_TILE = 512
OUTPROJ_ROW_TILE = N_ROWS // 16
SAMPLE_BLOCK = 8
CHUNK_SPLIT = 2
VMEM_CAPACITY = 64 * 1024 * 1024
VMEM_LIMIT = 60 * 1024 * 1024

F32 = jnp.float32
BF16 = jnp.bfloat16


def _rmsnorm(x, g):
    return x * lax.rsqrt(jnp.mean(x * x, axis=-1, keepdims=True) + EPS) * g


def _params(*semantics):
    return pltpu.CompilerParams(dimension_semantics=semantics, vmem_limit_bytes=VMEM_LIMIT)


def _poolsum_kernel(s_ref, o_ref):
    for gi, w in enumerate(POOL_WINDOWS):
        sl = slice(gi * POOL_GROUP_DIM, (gi + 1) * POOL_GROUP_DIM)
        acc = s_ref[0, POOL_BUF - 1, :, sl]
        for r in range(POOL_BUF - (w - 1), POOL_BUF - 1):
            acc = acc + s_ref[0, r, :, sl]
        o_ref[0, :, sl] = acc


def _pool_carry_sums(state_slots):
    n_layers = state_slots.shape[0]
    n_rows = 32
    return pl.pallas_call(
        _poolsum_kernel,
        grid=(n_layers, DEC_BATCH // n_rows),
        in_specs=[pl.BlockSpec((1, POOL_BUF, n_rows, D_MODEL), lambda j, n: (j, 0, n, 0))],
        out_specs=pl.BlockSpec((1, n_rows, D_MODEL), lambda j, n: (j, n, 0)),
        out_shape=jax.ShapeDtypeStruct((n_layers, DEC_BATCH, D_MODEL), F32),
        compiler_params=_params("arbitrary", "arbitrary"),
        name="pool_carry_sums",
    )(state_slots)


def _pool_state_kernel(s_ref, u0_ref, u1_ref, o_ref):
    r = pl.program_id(1)

    @pl.when(r < POOL_BUF - 1)
    def _():
        o_ref[0, 0] = s_ref[0, 0]

    @pl.when(r == POOL_BUF - 1)
    def _():
        u = jnp.where(pl.program_id(0) == 0, u0_ref[...], u1_ref[...])
        o_ref[0, 0] = u[N_META:N_META + DEC_BATCH, :]


def _pool_sample_states(state_slots, u_small0, u_small1):
    u_spec = pl.BlockSpec((SMALL_ROWS, D_MODEL), lambda j, r: (0, 0))
    return pl.pallas_call(
        _pool_state_kernel,
        grid=(state_slots.shape[0], POOL_BUF),
        in_specs=[
            pl.BlockSpec((1, 1, DEC_BATCH, D_MODEL), lambda j, r: (j, jnp.minimum(r + 1, POOL_BUF - 1), 0, 0)),
            u_spec,
            u_spec,
        ],
        out_specs=pl.BlockSpec((1, 1, DEC_BATCH, D_MODEL), lambda j, r: (j, r, 0, 0)),
        out_shape=jax.ShapeDtypeStruct(state_slots.shape, F32),
        compiler_params=_params("arbitrary", "arbitrary"),
        name="pool_sample_states",
    )(state_slots, u_small0, u_small1)


def _pool_kernel(xs_ref, xr_ref, carry_ref, g_ref, sc_ref, w_ref, o_ref, usmall_ref, utail_ref,
                 prev_scr, meta_scr, w_scr):
    t = pl.program_id(0)
    is_small = t == 0
    chunk = lax.rem(jnp.maximum(t - 1, 0), POOL_TILES_PER_BATCH)
    x = jnp.where(is_small, xs_ref[...], xr_ref[...])
    u = _rmsnorm(x, g_ref[...])

    @pl.when(is_small)
    def _():
        prev_scr[...] = jnp.zeros((16, D_MODEL), F32)
        meta_scr[...] = u[0:16, :]
        usmall_ref[...] = u
        w_scr[...] = w_ref[0].astype(BF16)

    @pl.when(jnp.logical_and(t > 0, chunk == 0))
    def _():
        prev_scr[...] = meta_scr[...]

    ext = jnp.concatenate([prev_scr[...], u], axis=0)
    row = lax.broadcasted_iota(jnp.int32, (POOL_TILE, 1), 0)
    meta_row = jnp.logical_and(is_small, row < N_META)
    sample_row = jnp.logical_and(is_small, row >= N_META)
    for gi, w in enumerate(POOL_WINDOWS):
        sl = slice(gi * POOL_GROUP_DIM, (gi + 1) * POOL_GROUP_DIM)
        s = ext[:, sl]
        shift = 1
        while shift < w:
            s = s + pltpu.roll(s, shift, 0)
            shift *= 2
        ug = u[:, sl]
        ws = jnp.where(sample_row, carry_ref[:, sl] + ug, s[16:, :])
        inv_cnt = 1.0 / jnp.where(meta_row, jnp.minimum(w, row + 1), w).astype(F32)
        d = (ws * inv_cnt - ug).astype(BF16)
        y = jnp.dot(d, w_scr[gi], preferred_element_type=F32)
        o_ref[:, sl] = x[:, sl] + y * sc_ref[:, sl]

    tail = u[POOL_TILE - 16:, :]
    prev_scr[...] = tail

    @pl.when(jnp.logical_and(t > 0, chunk == POOL_TILES_PER_BATCH - 1))
    def _():
        utail_ref[0] = tail


def _pool_layer(x_small, x_real, real_block0, carry_ext, norm_g, scale, pool_w, layer):
    n_tiles = N_ROWS // POOL_TILE
    row_spec = pl.BlockSpec((POOL_TILE, D_MODEL), lambda t: (t, 0))
    first_spec = pl.BlockSpec((POOL_TILE, D_MODEL), lambda t: (0, 0))
    vec_spec = pl.BlockSpec((1, D_MODEL), lambda t: (0, 0))
    n_groups = len(POOL_WINDOWS)
    return pl.pallas_call(
        _pool_kernel,
        grid=(n_tiles,),
        in_specs=[
            first_spec,
            pl.BlockSpec((POOL_TILE, D_MODEL), lambda t: (jnp.maximum(t - 1, 0) + real_block0, 0)),
            first_spec,
            vec_spec,
            vec_spec,
            pl.BlockSpec((1, n_groups, POOL_GROUP_DIM, POOL_GROUP_DIM), lambda t: (layer, 0, 0, 0)),
        ],
        out_specs=[
            row_spec,
            first_spec,
            pl.BlockSpec((1, 16, D_MODEL), lambda t: (jnp.maximum(t - 1, 0) // POOL_TILES_PER_BATCH, 0, 0)),
        ],
        out_shape=[
            jax.ShapeDtypeStruct((N_ROWS, D_MODEL), F32),
            jax.ShapeDtypeStruct((SMALL_ROWS, D_MODEL), F32),
            jax.ShapeDtypeStruct((BATCH, 16, D_MODEL), F32),
        ],
        scratch_shapes=[
            pltpu.VMEM((16, D_MODEL), F32),
            pltpu.VMEM((16, D_MODEL), F32),
            pltpu.VMEM((n_groups, POOL_GROUP_DIM, POOL_GROUP_DIM), BF16),
        ],
        compiler_params=_params("arbitrary"),
        name="pool_layer",
    )(x_small, x_real, carry_ext, norm_g.reshape(1, D_MODEL), scale.reshape(1, D_MODEL), pool_w)


WRAP_REAL_ROWS = ROW_TILE - SMALL_ROWS


def _consume_row_tile(x_hbm, x_scr, x_sems, consume, wrap=False):
    i = pl.program_id(0)
    j = pl.program_id(1)
    last = pl.num_programs(0) - 1

    def whole_tile_copy(tile):
        row0 = SMALL_ROWS + tile * ROW_TILE if wrap else tile * ROW_TILE
        rows = pl.ds(pl.multiple_of(row0, 8), ROW_TILE)
        return [pltpu.make_async_copy(x_hbm.at[rows, :], x_scr, x_sems.at[0])]

    def wrapped_tile_copy():
        return [
            pltpu.make_async_copy(x_hbm.at[pl.ds(N_ROWS - WRAP_REAL_ROWS, WRAP_REAL_ROWS), :],
                                  x_scr.at[pl.ds(0, WRAP_REAL_ROWS), :], x_sems.at[0]),
            pltpu.make_async_copy(x_hbm.at[pl.ds(0, SMALL_ROWS), :],
                                  x_scr.at[pl.ds(WRAP_REAL_ROWS, SMALL_ROWS), :], x_sems.at[1]),
        ]

    def for_tile(tile, action):
        if not wrap:
            for c in whole_tile_copy(tile):
                action(c)
            return

        @pl.when(tile < last)
        def _():
            for c in whole_tile_copy(tile):
                action(c)

        @pl.when(tile == last)
        def _():
            for c in wrapped_tile_copy():
                action(c)

    @pl.when(jnp.logical_and(i == 0, j == 0))
    def _():
        for_tile(i, lambda c: c.start())

    @pl.when(j == 0)
    def _():
        for_tile(i, lambda c: c.wait())
        consume()

    @pl.when(jnp.logical_and(j == 1, i < last))
    def _():
        for_tile(i + 1, lambda c: c.start())


def _mlp_kernel(*refs, final):
    if final:
        x_hbm, g_ref, wu_ref, wd_ref, gf_ref, o_ref, ys_ref, x_scr, u_scr, x_sems = refs
    else:
        x_hbm, g_ref, wu_ref, wd_ref, gf_ref, o_ref, x_scr, u_scr, x_sems = refs
    j = pl.program_id(1)

    def start_tile():
        x = x_scr[...]
        u_scr[...] = _rmsnorm(x, g_ref[...]).astype(BF16)
        o_ref[...] = x

    _consume_row_tile(x_hbm, x_scr, x_sems, start_tile, wrap=final)

    for c in range(FF_TILE // FF_CHUNK):
        ff = slice(c * FF_CHUNK, (c + 1) * FF_CHUNK)
        h = jnp.dot(u_scr[...], wu_ref[0, :, ff].astype(BF16), preferred_element_type=F32)
        h = jnp.square(jnp.maximum(h, 0.0)).astype(BF16)
        for n in range(D_MODEL // OUT_TILE):
            sl = slice(n * OUT_TILE, (n + 1) * OUT_TILE)
            o_ref[:, sl] += jnp.dot(h, wd_ref[0, ff, sl].astype(BF16), preferred_element_type=F32)

    if final:
        @pl.when(j == pl.num_programs(1) - 1)
        def _():
            o_ref[...] = _rmsnorm(o_ref[...], gf_ref[...])

            @pl.when(pl.program_id(0) == pl.num_programs(0) - 1)
            def _():
                rows = pl.ds(WRAP_REAL_ROWS + N_META, DEC_BATCH)
                copy = pltpu.make_async_copy(o_ref.at[rows, :], ys_ref, x_sems.at[2])
                copy.start()
                copy.wait()


def _mlp_layer(x, norm_g, w_up, w_down, layer, final_g, final):
    row_spec = pl.BlockSpec((ROW_TILE, D_MODEL), lambda i, j: (i, 0))
    vec_spec = pl.BlockSpec((1, D_MODEL), lambda i, j: (0, 0))
    if final:
        out_specs = [row_spec, pl.BlockSpec(memory_space=pl.ANY)]
        out_shape = [jax.ShapeDtypeStruct((BATCH * SEQ, D_MODEL), F32),
                     jax.ShapeDtypeStruct((DEC_BATCH, D_MODEL), F32)]
    else:
        out_specs, out_shape = row_spec, jax.ShapeDtypeStruct((N_ROWS, D_MODEL), F32)
    return pl.pallas_call(
        functools.partial(_mlp_kernel, final=final),
        grid=(N_ROWS // ROW_TILE, D_FF // FF_TILE),
        in_specs=[
            pl.BlockSpec(memory_space=pl.ANY),
            vec_spec,
            pl.BlockSpec((1, D_MODEL, FF_TILE), lambda i, j: (layer, 0, j)),
            pl.BlockSpec((1, FF_TILE, D_MODEL), lambda i, j: (layer, j, 0)),
            vec_spec,
        ],
        out_specs=out_specs,
        out_shape=out_shape,
        scratch_shapes=[
            pltpu.VMEM((ROW_TILE, D_MODEL), F32),
            pltpu.VMEM((ROW_TILE, D_MODEL), BF16),
            pltpu.SemaphoreType.DMA((3,)),
        ],
        compiler_params=pltpu.CompilerParams(dimension_semantics=("arbitrary", "arbitrary"),
                                             vmem_limit_bytes=VMEM_CAPACITY),
        name="mlp_layer",
    )(x, norm_g.reshape(1, D_MODEL), w_up, w_down, final_g.reshape(1, D_MODEL))


def _proj_kernel(x_hbm, g_ref, w_ref, cos_ref, sin_ref, o_ref, x_scr, u_scr, x_sems):
    j = pl.program_id(1)
    n_rope_tiles = 2 * D_MODEL // PROJ_TILE

    def start_tile():
        u_scr[...] = _rmsnorm(x_scr[...], g_ref[...]).astype(BF16)

    _consume_row_tile(x_hbm, x_scr, x_sems, start_tile)

    is_rope = j < n_rope_tiles
    k_scale = jnp.where(j >= n_rope_tiles // 2, RET_DK ** -0.5, 1.0).astype(F32)
    cos = jnp.where(is_rope, cos_ref[...] * k_scale, 1.0)
    sin = jnp.where(is_rope, sin_ref[...] * k_scale, 0.0)
    half = RET_DK // 2
    for hh in range(PROJ_TILE // RET_DK):
        acc = jnp.dot(u_scr[...], w_ref[0, :, hh * RET_DK:(hh + 1) * RET_DK].astype(BF16),
                      preferred_element_type=F32)
        x1 = acc[:, :half]
        x2 = acc[:, half:]
        o_ref[:, hh * RET_DK:hh * RET_DK + half] = (x1 * cos - x2 * sin).astype(BF16)
        o_ref[:, hh * RET_DK + half:(hh + 1) * RET_DK] = (x1 * sin + x2 * cos).astype(BF16)


def _ret_proj(x, norm_g, w_in, layer, cos, sin):
    vec_spec = pl.BlockSpec((1, D_MODEL), lambda i, j: (0, 0))
    rope_spec = pl.BlockSpec((ROW_TILE, RET_DK // 2), lambda i, j: (i, 0))
    return pl.pallas_call(
        _proj_kernel,
        grid=(N_ROWS // ROW_TILE, RET_IN // PROJ_TILE),
        in_specs=[
            pl.BlockSpec(memory_space=pl.ANY),
            vec_spec,
            pl.BlockSpec((1, D_MODEL, PROJ_TILE), lambda i, j: (layer, 0, j)),
            rope_spec,
            rope_spec,
        ],
        out_specs=pl.BlockSpec((ROW_TILE, PROJ_TILE), lambda i, j: (i, j)),
        out_shape=jax.ShapeDtypeStruct((N_ROWS, RET_IN), BF16),
        scratch_shapes=[
            pltpu.VMEM((ROW_TILE, D_MODEL), F32),
            pltpu.VMEM((ROW_TILE, D_MODEL), BF16),
            pltpu.SemaphoreType.DMA((1,)),
        ],
        compiler_params=pltpu.CompilerParams(dimension_semantics=("arbitrary", "arbitrary"),
                                             vmem_limit_bytes=VMEM_CAPACITY),
        name="ret_proj",
    )(x, norm_g.reshape(1, D_MODEL), w_in, cos, sin)


def _head_norm_gate(o, g):
    mu = jnp.mean(o, axis=-1, keepdims=True)
    oc = o - mu
    var = jnp.mean(oc * oc, axis=-1, keepdims=True)
    return oc * lax.rsqrt(var + EPS) * (g * jax.nn.sigmoid(g))


def _chunk_heads(heads, lg_ref, q_ref, k_ref, v_ref, g_ref, s_scr, y_ref, chunk, valid):
    row = lax.broadcasted_iota(jnp.int32, (chunk, 1), 0).astype(F32)
    col = lax.broadcasted_iota(jnp.int32, (1, chunk), 1).astype(F32)
    diff = row - col
    for h in heads:
        lg = lg_ref[h]
        qh = q_ref[:, h * RET_DK:(h + 1) * RET_DK]
        kh = k_ref[:, h * RET_DK:(h + 1) * RET_DK]
        vh = v_ref[:, h * RET_DV:(h + 1) * RET_DV]
        decay = jnp.where(diff >= 0, jnp.exp(jnp.maximum(diff, 0.0) * lg), 0.0)
        scores = lax.dot_general(qh, kh, (((1,), (1,)), ((), ())), preferred_element_type=F32) * decay
        inner = jnp.dot(scores.astype(BF16), vh, preferred_element_type=F32)
        st = s_scr[h]
        cross = jnp.dot(qh, st.astype(BF16), preferred_element_type=F32) * jnp.exp((row + 1.0) * lg)
        o = inner + cross
        kd = kh.astype(F32) * jnp.exp((valid - 1.0 - row) * lg)
        if valid < chunk:
            kd = jnp.where(row < valid, kd, 0.0)
        kd = kd.astype(BF16)
        s_dec = jnp.exp(jnp.full((1, RET_DV), valid, F32) * lg)
        s_scr[h] = st * s_dec + lax.dot_general(kd, vh, (((0,), (0,)), ((), ())),
                                               preferred_element_type=F32)
        gh = g_ref[:, h * RET_DV:(h + 1) * RET_DV].astype(F32)
        y_ref[:, h * RET_DV:(h + 1) * RET_DV] = _head_norm_gate(o, gh).astype(BF16)


def _sample_head(h, nb, lg_ref, qt_ref, kt_ref, v_ref, g_ref, s_ref, so_ref, ysm_scr):
    lg = lg_ref[h]
    gamma = jnp.exp(jnp.full((1, 128), 1.0, F32) * lg)
    sample = lax.broadcasted_iota(jnp.int32, (DEC_BATCH, 128), 0)
    row8 = lax.broadcasted_iota(jnp.int32, (SAMPLE_BLOCK, 1), 0)
    qt = qt_ref[0]
    kt = kt_ref[0]
    o_all = jnp.zeros((SAMPLE_BLOCK, RET_DV), F32)
    for n in range(SAMPLE_BLOCK):
        onehot = (sample == nb * SAMPLE_BLOCK + n).astype(BF16)
        qb = jnp.dot(qt, onehot, preferred_element_type=F32)
        kb = jnp.dot(kt, onehot, preferred_element_type=F32)
        v_row = v_ref[n:n + 1, :]
        parts = []
        for vt in range(RET_DV // 128):
            sl = slice(vt * 128, (vt + 1) * 128)
            s_new = s_ref[0, n, 0, :, sl] * gamma + kb * v_row[:, sl]
            so_ref[0, n, 0, :, sl] = s_new
            parts.append(jnp.sum(s_new * qb, axis=0, keepdims=True))
        o_n = jnp.concatenate(parts, axis=1)
        o_all = jnp.where(row8 == n, o_n, o_all)
    rows = pl.ds(pl.multiple_of(N_META + nb * SAMPLE_BLOCK, SAMPLE_BLOCK), SAMPLE_BLOCK)
    ysm_scr[h, rows, :] = _head_norm_gate(o_all, g_ref[...])


def _ret_kernel(*refs, chunk, valid, n_chunks, n_seq, fused, aliased):
    if not fused:
        lg_ref, q_ref, k_ref, v_ref, g_ref, s0_ref, y_ref, sout_ref, s_scr = refs
        s_scr[...] = s0_ref[...]
        _chunk_heads(range(RET_HEADS), lg_ref, q_ref, k_ref, v_ref, g_ref, s_scr, y_ref, chunk, valid)
        sout_ref[0] = s_scr[...]
        return

    (lg_ref, q_ref, k_ref, v_ref, g_ref, s0_ref, ymeta_ref, qt_ref, kt_ref, vs_ref, gs_ref,
     st_ref) = refs[:12]
    y_ref, sout_ref, sto_ref, s_scr, ysm_scr = refs[12 + (1 if aliased else 0):]
    s = pl.program_id(0)
    n_main = n_seq * n_chunks * CHUNK_SPLIT
    heads_per_step = RET_HEADS // CHUNK_SPLIT

    @pl.when(s < n_main)
    def _():
        part = lax.rem(s, CHUNK_SPLIT)
        c = lax.rem(s // CHUNK_SPLIT, n_chunks)

        @pl.when(s == 0)
        def _():
            ysm_scr[...] = jnp.zeros(ysm_scr.shape, F32)
            for h in range(RET_HEADS):
                ysm_scr[h, 0:N_META, :] = ymeta_ref[0:N_META, h * RET_DV:(h + 1) * RET_DV].astype(F32)

        @pl.when(jnp.logical_and(c == 0, part == 0))
        def _():
            s_scr[...] = s0_ref[...]

        for p in range(CHUNK_SPLIT):
            @pl.when(part == p)
            def _(p=p):
                _chunk_heads(range(p * heads_per_step, (p + 1) * heads_per_step),
                             lg_ref, q_ref, k_ref, v_ref, g_ref, s_scr, y_ref, chunk, valid)

        @pl.when(jnp.logical_and(c == n_chunks - 1, part == CHUNK_SPLIT - 1))
        def _():
            sout_ref[0] = s_scr[...]

        n_blocks = DEC_BATCH // SAMPLE_BLOCK
        _sample_head(s // n_blocks, lax.rem(s, n_blocks), lg_ref, qt_ref, kt_ref, vs_ref, gs_ref,
                     st_ref, sto_ref, ysm_scr)

    @pl.when(s >= n_main)
    def _():
        rows = pl.ds(pl.multiple_of((s - n_main) * chunk, chunk), chunk)
        for h in range(RET_HEADS):
            y_ref[:, h * RET_DV:(h + 1) * RET_DV] = ysm_scr[h, rows, :].astype(BF16)


def _retention(log_gamma, qkvg, s0, sample, *, n_seq, n_chunks, chunk, valid):
    k_col = 1
    v_col = 2 * D_MODEL // RET_VDIM
    fused = sample is not None
    if not fused:
        n_steps, n_rows = n_seq * n_chunks, n_seq * n_chunks * chunk
        blk = lambda s: s
        y_blk = lambda s: s
        seq = lambda s: s // n_chunks
    else:
        y_meta, qt, kt, vs, gs, state_ret, prev_out, layer = sample
        n_small = SMALL_ROWS // chunk
        n_main = n_seq * n_chunks * CHUNK_SPLIT
        assert n_main == RET_HEADS * (DEC_BATCH // SAMPLE_BLOCK)
        n_steps, n_rows = n_main + n_small, N_ROWS
        blk = lambda s: n_small + jnp.minimum(s, n_main - 1) // CHUNK_SPLIT
        y_blk = lambda s: jnp.where(s < n_main, n_small + s // CHUNK_SPLIT, s - n_main)
        seq = lambda s: jnp.minimum(s, n_main - 1) // (CHUNK_SPLIT * n_chunks)
    in_specs = [
        pl.BlockSpec(memory_space=pltpu.SMEM),
        pl.BlockSpec((chunk, D_MODEL), lambda s: (blk(s), 0)),
        pl.BlockSpec((chunk, D_MODEL), lambda s: (blk(s), k_col)),
        pl.BlockSpec((chunk, RET_VDIM), lambda s: (blk(s), v_col)),
        pl.BlockSpec((chunk, RET_VDIM), lambda s: (blk(s), v_col + 1)),
        pl.BlockSpec((RET_HEADS, RET_DK, RET_DV), lambda s: (0, 0, 0)),
    ]
    args = [log_gamma, qkvg, qkvg, qkvg, qkvg, s0]
    out_specs = [
        pl.BlockSpec((chunk, RET_VDIM), lambda s: (y_blk(s), 0)),
        pl.BlockSpec((1, RET_HEADS, RET_DK, RET_DV), lambda s: (seq(s), 0, 0, 0)),
    ]
    out_shape = [
        jax.ShapeDtypeStruct((n_rows, RET_VDIM), BF16),
        jax.ShapeDtypeStruct((n_seq, RET_HEADS, RET_DK, RET_DV), F32),
    ]
    scratch = [pltpu.VMEM((RET_HEADS, RET_DK, RET_DV), F32)]
    aliases = {}
    if fused:
        n_blocks = DEC_BATCH // SAMPLE_BLOCK
        head = lambda s: jnp.minimum(s, n_main - 1) // n_blocks
        sblk = lambda s: lax.rem(jnp.minimum(s, n_main - 1), n_blocks)
        state_spec = pl.BlockSpec((1, SAMPLE_BLOCK, 1, RET_DK, RET_DV),
                                  lambda s: (layer, sblk(s), head(s), 0, 0))
        t_spec = pl.BlockSpec((1, RET_DK, DEC_BATCH), lambda s: (head(s), 0, 0))
        vg_spec = pl.BlockSpec((SAMPLE_BLOCK, RET_DV), lambda s: (sblk(s), head(s)))
        in_specs += [pl.BlockSpec((chunk, RET_VDIM), lambda s: (0, 0)), t_spec, t_spec, vg_spec, vg_spec,
                     state_spec]
        args += [y_meta, qt, kt, vs, gs, state_ret]
        if prev_out is not None:
            aliases = {len(args): 2}
            in_specs.append(pl.BlockSpec(memory_space=pl.ANY))
            args.append(prev_out)
        out_specs.append(state_spec)
        out_shape.append(jax.ShapeDtypeStruct(state_ret.shape, F32))
        scratch.append(pltpu.VMEM((RET_HEADS, SMALL_ROWS, RET_DV), F32))
    return pl.pallas_call(
        functools.partial(_ret_kernel, chunk=chunk, valid=valid, n_chunks=n_chunks, n_seq=n_seq,
                          fused=fused, aliased=bool(aliases)),
        grid=(n_steps,),
        in_specs=in_specs,
        out_specs=out_specs,
        out_shape=out_shape,
        scratch_shapes=scratch,
        input_output_aliases=aliases,
        compiler_params=_params("arbitrary"),
        name="retention",
    )(*args)


def _outproj_kernel(x_ref, y_ref, w_ref, o_ref):
    for n in range(D_MODEL // OUT_TILE):
        sl = slice(n * OUT_TILE, (n + 1) * OUT_TILE)
        o_ref[:, sl] = x_ref[:, sl] + jnp.dot(y_ref[...], w_ref[0, :, sl].astype(BF16),
                                              preferred_element_type=F32)


def _ret_outproj(x, y, w_out, layer):
    x_spec = pl.BlockSpec((OUTPROJ_ROW_TILE, D_MODEL), lambda i: (i, 0))
    return pl.pallas_call(
        _outproj_kernel,
        grid=(N_ROWS // OUTPROJ_ROW_TILE,),
        in_specs=[
            x_spec,
            pl.BlockSpec((OUTPROJ_ROW_TILE, RET_VDIM), lambda i: (i, 0)),
            pl.BlockSpec((1, RET_VDIM, D_MODEL), lambda i: (layer, 0, 0), pipeline_mode=pl.Buffered(1)),
        ],
        out_specs=x_spec,
        out_shape=jax.ShapeDtypeStruct((N_ROWS, D_MODEL), F32),
        compiler_params=_params("arbitrary"),
        name="ret_outproj",
    )(x, y, w_out)


def _rope_tables():
    pos = np.zeros((N_ROWS,), np.int32)
    pos[:N_META] = np.arange(N_META)
    pos[N_META:N_META + DEC_BATCH] = PAST_LEN
    pos[SMALL_ROWS:] = np.tile(N_META + np.arange(SEQ), BATCH)
    half = RET_DK // 2
    inv = ROPE_BASE ** (-jnp.arange(half, dtype=F32) / half)
    ang = jnp.asarray(pos).astype(F32)[:, None] * inv[None, :]
    return jnp.cos(ang), jnp.sin(ang)


def kernel(x_prompt, x_sample, state_pool, state_ret, meta_tokens, pool_norm, pool_w, pool_scale,
           ret_norm, ret_w_in, ret_w_out, mlp_norm, mlp_w_up, mlp_w_down, final_norm):
    assert x_prompt.shape == (BATCH, SEQ, D_MODEL) and x_sample.shape == (DEC_BATCH, 1, D_MODEL)
    pad_rows = SMALL_ROWS - N_META - DEC_BATCH
    x_small = jnp.concatenate([
        meta_tokens.astype(F32),
        x_sample.reshape(DEC_BATCH, D_MODEL),
        jnp.zeros((pad_rows, D_MODEL), F32),
    ], axis=0)
    x = None
    log_gamma = jnp.log1p(-jnp.exp2(-5.0 - jnp.arange(RET_HEADS, dtype=F32)))
    cos, sin = _rope_tables()
    state_slots = state_pool.transpose(0, 2, 1, 3)
    carry_sums = _pool_carry_sums(state_slots)

    pool_p, pool_s, ret_p = [], [], []
    ret_s = None
    for i in range(DEPTH):
        j = i // 2
        if i % 2 == 0:
            carry_ext = jnp.pad(carry_sums[j], ((N_META, pad_rows), (0, 0)))
            if x is None:
                xs, xr, block0 = x_small, x_prompt.reshape(BATCH * SEQ, D_MODEL), 0
            else:
                xs, xr, block0 = x, x, SMALL_ROWS // POOL_TILE
            x, u_small, u_tail = _pool_layer(xs, xr, block0, carry_ext, pool_norm[j], pool_scale[j], pool_w, j)
            pool_p.append(u_tail[:, 1:])
            pool_s.append(u_small)
        else:
            qkvg = _ret_proj(x, ret_norm[j], ret_w_in, j, cos, sin)
            zero_state = jnp.zeros((RET_HEADS, RET_DK, RET_DV), F32)
            y_meta, s_meta = _retention(log_gamma, qkvg, zero_state, None,
                                        n_seq=1, n_chunks=1, chunk=RET_CHUNK, valid=N_META)
            qkvg_s = qkvg[N_META:N_META + DEC_BATCH]
            qt = qkvg_s[:, :D_MODEL].reshape(DEC_BATCH, RET_HEADS, RET_DK).transpose(1, 2, 0)
            kt = qkvg_s[:, D_MODEL:2 * D_MODEL].reshape(DEC_BATCH, RET_HEADS, RET_DK).transpose(1, 2, 0)
            sample = (y_meta, qt, kt, qkvg_s[:, 2 * D_MODEL:2 * D_MODEL + RET_VDIM].astype(F32),
                      qkvg_s[:, 2 * D_MODEL + RET_VDIM:].astype(F32), state_ret, ret_s, j)
            y, s_real, ret_s = _retention(log_gamma, qkvg, s_meta[0], sample, n_seq=BATCH,
                                          n_chunks=SEQ // RET_CHUNK, chunk=RET_CHUNK, valid=RET_CHUNK)
            ret_p.append(s_real)
            x = _ret_outproj(x, y, ret_w_out, j)
        x = _mlp_layer(x, mlp_norm[i], mlp_w_up, mlp_w_down, i, final_norm, i == DEPTH - 1)

    y_prompt = x[0].reshape(BATCH, SEQ, D_MODEL)
    y_sample = x[1].reshape(DEC_BATCH, 1, D_MODEL)
    new_pool_sample = _pool_sample_states(state_slots, *pool_s).transpose(0, 2, 1, 3)
    return (y_prompt, y_sample, jnp.stack(pool_p, axis=0), new_pool_sample,
            jnp.stack(ret_p, axis=0), ret_s)
```

```python
import functools

import numpy as np
import jax
import jax.numpy as jnp
from jax import lax
from jax.experimental import pallas as pl
from jax.experimental.pallas import tpu as pltpu

D_MODEL = 2048
BATCH = 4
SEQ = 2048
DEPTH = 4
DEC_BATCH = 128
PAST_LEN = 16384
N_META = 16
POOL_WINDOWS = (2, 4, 8, 16)
POOL_GROUP_DIM = D_MODEL // len(POOL_WINDOWS)
POOL_BUF = max(POOL_WINDOWS) - 1
RET_HEADS = 8
RET_DK = D_MODEL // RET_HEADS
RET_DV = 2 * RET_DK
RET_VDIM = RET_HEADS * RET_DV
RET_IN = 2 * D_MODEL + 2 * RET_VDIM
RET_CHUNK = 128
ROPE_BASE = 10000.0
D_FF = 4 * D_MODEL
EPS = 1e-6

SMALL_ROWS = 256
N_ROWS = SMALL_ROWS + BATCH * SEQ
ROW_TILE = N_ROWS // 8
POOL_TILE = 256
POOL_TILES_PER_BATCH = SEQ // POOL_TILE
FF_TILE = 1024
FF_CHUNK = 512
PROJ_TILE = 2048
OUT_TILE = 512
OUTPROJ_ROW_TILE = N_ROWS // 16
SAMPLE_BLOCK = 8
CHUNK_SPLIT = 2
VMEM_CAPACITY = 64 * 1024 * 1024
VMEM_LIMIT = 60 * 1024 * 1024

F32 = jnp.float32
BF16 = jnp.bfloat16


def _rmsnorm(x, g):
    return x * lax.rsqrt(jnp.mean(x * x, axis=-1, keepdims=True) + EPS) * g


def _params(*semantics):
    return pltpu.CompilerParams(dimension_semantics=semantics, vmem_limit_bytes=VMEM_LIMIT)


def _poolsum_kernel(s_ref, o_ref):
    for gi, w in enumerate(POOL_WINDOWS):
        sl = slice(gi * POOL_GROUP_DIM, (gi + 1) * POOL_GROUP_DIM)
        acc = s_ref[0, POOL_BUF - 1, :, sl]
        for r in range(POOL_BUF - (w - 1), POOL_BUF - 1):
            acc = acc + s_ref[0, r, :, sl]
        o_ref[0, :, sl] = acc


def _pool_carry_sums(state_slots):
    n_layers = state_slots.shape[0]
    n_rows = 32
    return pl.pallas_call(
        _poolsum_kernel,
        grid=(n_layers, DEC_BATCH // n_rows),
        in_specs=[pl.BlockSpec((1, POOL_BUF, n_rows, D_MODEL), lambda j, n: (j, 0, n, 0))],
        out_specs=pl.BlockSpec((1, n_rows, D_MODEL), lambda j, n: (j, n, 0)),
        out_shape=jax.ShapeDtypeStruct((n_layers, DEC_BATCH, D_MODEL), F32),
        compiler_params=_params("arbitrary", "arbitrary"),
        name="pool_carry_sums",
    )(state_slots)


def _pool_state_kernel(s_ref, u0_ref, u1_ref, o_ref):
    r = pl.program_id(1)

    @pl.when(r < POOL_BUF - 1)
    def _():
        o_ref[0, 0] = s_ref[0, 0]

    @pl.when(r == POOL_BUF - 1)
    def _():
        u = jnp.where(pl.program_id(0) == 0, u0_ref[...], u1_ref[...])
        o_ref[0, 0] = u[N_META:N_META + DEC_BATCH, :]


def _pool_sample_states(state_slots, u_small0, u_small1):
    u_spec = pl.BlockSpec((SMALL_ROWS, D_MODEL), lambda j, r: (0, 0))
    return pl.pallas_call(
        _pool_state_kernel,
        grid=(state_slots.shape[0], POOL_BUF),
        in_specs=[
            pl.BlockSpec((1, 1, DEC_BATCH, D_MODEL), lambda j, r: (j, jnp.minimum(r + 1, POOL_BUF - 1), 0, 0)),
            u_spec,
            u_spec,
        ],
        out_specs=pl.BlockSpec((1, 1, DEC_BATCH, D_MODEL), lambda j, r: (j, r, 0, 0)),
        out_shape=jax.ShapeDtypeStruct(state_slots.shape, F32),
        compiler_params=_params("arbitrary", "arbitrary"),
        name="pool_sample_states",
    )(state_slots, u_small0, u_small1)


def _pool_kernel(xs_ref, xr_ref, carry_ref, g_ref, sc_ref, w_ref, o_ref, usmall_ref, utail_ref,
                 prev_scr, meta_scr, w_scr):
    t = pl.program_id(0)
    is_small = t == 0
    chunk = lax.rem(jnp.maximum(t - 1, 0), POOL_TILES_PER_BATCH)
    x = jnp.where(is_small, xs_ref[...], xr_ref[...])
    u = _rmsnorm(x, g_ref[...])

    @pl.when(is_small)
    def _():
        prev_scr[...] = jnp.zeros((16, D_MODEL), F32)
        meta_scr[...] = u[0:16, :]
        usmall_ref[...] = u
        w_scr[...] = w_ref[0].astype(BF16)

    @pl.when(jnp.logical_and(t > 0, chunk == 0))
    def _():
        prev_scr[...] = meta_scr[...]

    ext = jnp.concatenate([prev_scr[...], u], axis=0)
    row = lax.broadcasted_iota(jnp.int32, (POOL_TILE, 1), 0)
    meta_row = jnp.logical_and(is_small, row < N_META)
    sample_row = jnp.logical_and(is_small, row >= N_META)
    for gi, w in enumerate(POOL_WINDOWS):
        sl = slice(gi * POOL_GROUP_DIM, (gi + 1) * POOL_GROUP_DIM)
        s = ext[:, sl]
        shift = 1
        while shift < w:
            s = s + pltpu.roll(s, shift, 0)
            shift *= 2
        ug = u[:, sl]
        ws = jnp.where(sample_row, carry_ref[:, sl] + ug, s[16:, :])
        inv_cnt = 1.0 / jnp.where(meta_row, jnp.minimum(w, row + 1), w).astype(F32)
        d = (ws * inv_cnt - ug).astype(BF16)
        y = jnp.dot(d, w_scr[gi], preferred_element_type=F32)
        o_ref[:, sl] = x[:, sl] + y * sc_ref[:, sl]

    tail = u[POOL_TILE - 16:, :]
    prev_scr[...] = tail

    @pl.when(jnp.logical_and(t > 0, chunk == POOL_TILES_PER_BATCH - 1))
    def _():
        utail_ref[0] = tail


def _pool_layer(x_small, x_real, real_block0, carry_ext, norm_g, scale, pool_w, layer):
    n_tiles = N_ROWS // POOL_TILE
    row_spec = pl.BlockSpec((POOL_TILE, D_MODEL), lambda t: (t, 0))
    first_spec = pl.BlockSpec((POOL_TILE, D_MODEL), lambda t: (0, 0))
    vec_spec = pl.BlockSpec((1, D_MODEL), lambda t: (0, 0))
    n_groups = len(POOL_WINDOWS)
    return pl.pallas_call(
        _pool_kernel,
        grid=(n_tiles,),
        in_specs=[
            first_spec,
            pl.BlockSpec((POOL_TILE, D_MODEL), lambda t: (jnp.maximum(t - 1, 0) + real_block0, 0)),
            first_spec,
            vec_spec,
            vec_spec,
            pl.BlockSpec((1, n_groups, POOL_GROUP_DIM, POOL_GROUP_DIM), lambda t: (layer, 0, 0, 0)),
        ],
        out_specs=[
            row_spec,
            first_spec,
            pl.BlockSpec((1, 16, D_MODEL), lambda t: (jnp.maximum(t - 1, 0) // POOL_TILES_PER_BATCH, 0, 0)),
        ],
        out_shape=[
            jax.ShapeDtypeStruct((N_ROWS, D_MODEL), F32),
            jax.ShapeDtypeStruct((SMALL_ROWS, D_MODEL), F32),
            jax.ShapeDtypeStruct((BATCH, 16, D_MODEL), F32),
        ],
        scratch_shapes=[
            pltpu.VMEM((16, D_MODEL), F32),
            pltpu.VMEM((16, D_MODEL), F32),
            pltpu.VMEM((n_groups, POOL_GROUP_DIM, POOL_GROUP_DIM), BF16),
        ],
        compiler_params=_params("arbitrary"),
        name="pool_layer",
    )(x_small, x_real, carry_ext, norm_g.reshape(1, D_MODEL), scale.reshape(1, D_MODEL), pool_w)


WRAP_REAL_ROWS = ROW_TILE - SMALL_ROWS


def _consume_row_tile(x_hbm, x_scr, x_sems, consume, wrap=False):
    i = pl.program_id(0)
    j = pl.program_id(1)
    last = pl.num_programs(0) - 1

    def whole_tile_copy(tile):
        row0 = SMALL_ROWS + tile * ROW_TILE if wrap else tile * ROW_TILE
        rows = pl.ds(pl.multiple_of(row0, 8), ROW_TILE)
        return [pltpu.make_async_copy(x_hbm.at[rows, :], x_scr, x_sems.at[0])]

    def wrapped_tile_copy():
        return [
            pltpu.make_async_copy(x_hbm.at[pl.ds(N_ROWS - WRAP_REAL_ROWS, WRAP_REAL_ROWS), :],
                                  x_scr.at[pl.ds(0, WRAP_REAL_ROWS), :], x_sems.at[0]),
            pltpu.make_async_copy(x_hbm.at[pl.ds(0, SMALL_ROWS), :],
                                  x_scr.at[pl.ds(WRAP_REAL_ROWS, SMALL_ROWS), :], x_sems.at[1]),
        ]

    def for_tile(tile, action):
        if not wrap:
            for c in whole_tile_copy(tile):
                action(c)
            return

        @pl.when(tile < last)
        def _():
            for c in whole_tile_copy(tile):
                action(c)

        @pl.when(tile == last)
        def _():
            for c in wrapped_tile_copy():
                action(c)

    @pl.when(jnp.logical_and(i == 0, j == 0))
    def _():
        for_tile(i, lambda c: c.start())

    @pl.when(j == 0)
    def _():
        for_tile(i, lambda c: c.wait())
        consume()

    @pl.when(jnp.logical_and(j == 1, i < last))
    def _():
        for_tile(i + 1, lambda c: c.start())


def _mlp_kernel(*refs, final):
    if final:
        x_hbm, g_ref, wu_ref, wd_ref, gf_ref, o_ref, ys_ref, x_scr, u_scr, x_sems = refs
    else:
        x_hbm, g_ref, wu_ref, wd_ref, gf_ref, o_ref, x_scr, u_scr, x_sems = refs
    j = pl.program_id(1)

    def start_tile():
        x = x_scr[...]
        u_scr[...] = _rmsnorm(x, g_ref[...]).astype(BF16)
        o_ref[...] = x

    _consume_row_tile(x_hbm, x_scr, x_sems, start_tile, wrap=final)

    for c in range(FF_TILE // FF_CHUNK):
        ff = slice(c * FF_CHUNK, (c + 1) * FF_CHUNK)
        h = jnp.dot(u_scr[...], wu_ref[0, :, ff].astype(BF16), preferred_element_type=F32)
        h = jnp.square(jnp.maximum(h, 0.0)).astype(BF16)
        for n in range(D_MODEL // OUT_TILE):
            sl = slice(n * OUT_TILE, (n + 1) * OUT_TILE)
            o_ref[:, sl] += jnp.dot(h, wd_ref[0, ff, sl].astype(BF16), preferred_element_type=F32)

    if final:
        @pl.when(j == pl.num_programs(1) - 1)
        def _():
            o_ref[...] = _rmsnorm(o_ref[...], gf_ref[...])

            @pl.when(pl.program_id(0) == pl.num_programs(0) - 1)
            def _():
                rows = pl.ds(WRAP_REAL_ROWS + N_META, DEC_BATCH)
                copy = pltpu.make_async_copy(o_ref.at[rows, :], ys_ref, x_sems.at[2])
                copy.start()
                copy.wait()


def _mlp_layer(x, norm_g, w_up, w_down, layer, final_g, final):
    row_spec = pl.BlockSpec((ROW_TILE, D_MODEL), lambda i, j: (i, 0))
    vec_spec = pl.BlockSpec((1, D_MODEL), lambda i, j: (0, 0))
    if final:
        out_specs = [row_spec, pl.BlockSpec(memory_space=pl.ANY)]
        out_shape = [jax.ShapeDtypeStruct((BATCH * SEQ, D_MODEL), F32),
                     jax.ShapeDtypeStruct((DEC_BATCH, D_MODEL), F32)]
    else:
        out_specs, out_shape = row_spec, jax.ShapeDtypeStruct((N_ROWS, D_MODEL), F32)
    return pl.pallas_call(
        functools.partial(_mlp_kernel, final=final),
        grid=(N_ROWS // ROW_TILE, D_FF // FF_TILE),
        in_specs=[
            pl.BlockSpec(memory_space=pl.ANY),
            vec_spec,
            pl.BlockSpec((1, D_MODEL, FF_TILE), lambda i, j: (layer, 0, j)),
            pl.BlockSpec((1, FF_TILE, D_MODEL), lambda i, j: (layer, j, 0)),
            vec_spec,
        ],
        out_specs=out_specs,
        out_shape=out_shape,
        scratch_shapes=[
            pltpu.VMEM((ROW_TILE, D_MODEL), F32),
            pltpu.VMEM((ROW_TILE, D_MODEL), BF16),
            pltpu.SemaphoreType.DMA((3,)),
        ],
        compiler_params=pltpu.CompilerParams(dimension_semantics=("arbitrary", "arbitrary"),
                                             vmem_limit_bytes=VMEM_CAPACITY),
        name="mlp_layer",
    )(x, norm_g.reshape(1, D_MODEL), w_up, w_down, final_g.reshape(1, D_MODEL))


def _proj_kernel(x_hbm, g_ref, w_ref, cos_ref, sin_ref, o_ref, x_scr, u_scr, x_sems):
    j = pl.program_id(1)
    n_rope_tiles = 2 * D_MODEL // PROJ_TILE

    def start_tile():
        u_scr[...] = _rmsnorm(x_scr[...], g_ref[...]).astype(BF16)

    _consume_row_tile(x_hbm, x_scr, x_sems, start_tile)

    is_rope = j < n_rope_tiles
    k_scale = jnp.where(j >= n_rope_tiles // 2, RET_DK ** -0.5, 1.0).astype(F32)
    cos = jnp.where(is_rope, cos_ref[...] * k_scale, 1.0)
    sin = jnp.where(is_rope, sin_ref[...] * k_scale, 0.0)
    half = RET_DK // 2
    for hh in range(PROJ_TILE // RET_DK):
        acc = jnp.dot(u_scr[...], w_ref[0, :, hh * RET_DK:(hh + 1) * RET_DK].astype(BF16),
                      preferred_element_type=F32)
        x1 = acc[:, :half]
        x2 = acc[:, half:]
        o_ref[:, hh * RET_DK:hh * RET_DK + half] = (x1 * cos - x2 * sin).astype(BF16)
        o_ref[:, hh * RET_DK + half:(hh + 1) * RET_DK] = (x1 * sin + x2 * cos).astype(BF16)


def _ret_proj(x, norm_g, w_in, layer, cos, sin):
    vec_spec = pl.BlockSpec((1, D_MODEL), lambda i, j: (0, 0))
    rope_spec = pl.BlockSpec((ROW_TILE, RET_DK // 2), lambda i, j: (i, 0))
    return pl.pallas_call(
        _proj_kernel,
        grid=(N_ROWS // ROW_TILE, RET_IN // PROJ_TILE),
        in_specs=[
            pl.BlockSpec(memory_space=pl.ANY),
            vec_spec,
            pl.BlockSpec((1, D_MODEL, PROJ_TILE), lambda i, j: (layer, 0, j)),
            rope_spec,
            rope_spec,
        ],
        out_specs=pl.BlockSpec((ROW_TILE, PROJ_TILE), lambda i, j: (i, j)),
        out_shape=jax.ShapeDtypeStruct((N_ROWS, RET_IN), BF16),
        scratch_shapes=[
            pltpu.VMEM((ROW_TILE, D_MODEL), F32),
            pltpu.VMEM((ROW_TILE, D_MODEL), BF16),
            pltpu.SemaphoreType.DMA((1,)),
        ],
        compiler_params=pltpu.CompilerParams(dimension_semantics=("arbitrary", "arbitrary"),
                                             vmem_limit_bytes=VMEM_CAPACITY),
        name="ret_proj",
    )(x, norm_g.reshape(1, D_MODEL), w_in, cos, sin)


def _head_norm_gate(o, g):
    mu = jnp.mean(o, axis=-1, keepdims=True)
    oc = o - mu
    var = jnp.mean(oc * oc, axis=-1, keepdims=True)
    return oc * lax.rsqrt(var + EPS) * (g * jax.nn.sigmoid(g))


def _chunk_heads(heads, lg_ref, q_ref, k_ref, v_ref, g_ref, s_scr, y_ref, chunk, valid):
    row = lax.broadcasted_iota(jnp.int32, (chunk, 1), 0).astype(F32)
    col = lax.broadcasted_iota(jnp.int32, (1, chunk), 1).astype(F32)
    diff = row - col
    for h in heads:
        lg = lg_ref[h]
        qh = q_ref[:, h * RET_DK:(h + 1) * RET_DK]
        kh = k_ref[:, h * RET_DK:(h + 1) * RET_DK]
        vh = v_ref[:, h * RET_DV:(h + 1) * RET_DV]
        decay = jnp.where(diff >= 0, jnp.exp(jnp.maximum(diff, 0.0) * lg), 0.0)
        scores = lax.dot_general(qh, kh, (((1,), (1,)), ((), ())), preferred_element_type=F32) * decay
        inner = jnp.dot(scores.astype(BF16), vh, preferred_element_type=F32)
        st = s_scr[h]
        cross = jnp.dot(qh, st.astype(BF16), preferred_element_type=F32) * jnp.exp((row + 1.0) * lg)
        o = inner + cross
        kd = kh.astype(F32) * jnp.exp((valid - 1.0 - row) * lg)
        if valid < chunk:
            kd = jnp.where(row < valid, kd, 0.0)
        kd = kd.astype(BF16)
        s_dec = jnp.exp(jnp.full((1, RET_DV), valid, F32) * lg)
        s_scr[h] = st * s_dec + lax.dot_general(kd, vh, (((0,), (0,)), ((), ())),
                                               preferred_element_type=F32)
        gh = g_ref[:, h * RET_DV:(h + 1) * RET_DV].astype(F32)
        y_ref[:, h * RET_DV:(h + 1) * RET_DV] = _head_norm_gate(o, gh).astype(BF16)


def _sample_head(h, nb, lg_ref, qt_ref, kt_ref, v_ref, g_ref, s_ref, so_ref, ysm_scr):
    lg = lg_ref[h]
    gamma = jnp.exp(jnp.full((1, 128), 1.0, F32) * lg)
    sample = lax.broadcasted_iota(jnp.int32, (DEC_BATCH, 128), 0)
    row8 = lax.broadcasted_iota(jnp.int32, (SAMPLE_BLOCK, 1), 0)
    qt = qt_ref[0]
    kt = kt_ref[0]
    o_all = jnp.zeros((SAMPLE_BLOCK, RET_DV), F32)
    for n in range(SAMPLE_BLOCK):
        onehot = (sample == nb * SAMPLE_BLOCK + n).astype(BF16)
        qb = jnp.dot(qt, onehot, preferred_element_type=F32)
        kb = jnp.dot(kt, onehot, preferred_element_type=F32)
        v_row = v_ref[n:n + 1, :]
        parts = []
        for vt in range(RET_DV // 128):
            sl = slice(vt * 128, (vt + 1) * 128)
            s_new = s_ref[n, :, sl] * gamma + kb * v_row[:, sl]
            so_ref[0, n, 0, :, sl] = s_new
            parts.append(jnp.sum(s_new * qb, axis=0, keepdims=True))
        o_n = jnp.concatenate(parts, axis=1)
        o_all = jnp.where(row8 == n, o_n, o_all)
    rows = pl.ds(pl.multiple_of(N_META + nb * SAMPLE_BLOCK, SAMPLE_BLOCK), SAMPLE_BLOCK)
    ysm_scr[h, rows, :] = _head_norm_gate(o_all, g_ref[...])


STATE_RING = 3


def _ret_kernel(*refs, chunk, valid, n_chunks, n_seq, fused, aliased, layer):
    if not fused:
        lg_ref, q_ref, k_ref, v_ref, g_ref, s0_ref, y_ref, sout_ref, s_scr = refs
        s_scr[...] = s0_ref[...]
        _chunk_heads(range(RET_HEADS), lg_ref, q_ref, k_ref, v_ref, g_ref, s_scr, y_ref, chunk, valid)
        sout_ref[0] = s_scr[...]
        return

    (lg_ref, q_ref, k_ref, v_ref, g_ref, s0_ref, ymeta_ref, qt_ref, kt_ref, vs_ref, gs_ref,
     st_hbm) = refs[:12]
    y_ref, sout_ref, sto_ref, s_scr, ysm_scr, ring, ring_sems = refs[12 + (1 if aliased else 0):]
    s = pl.program_id(0)
    n_main = n_seq * n_chunks * CHUNK_SPLIT
    heads_per_step = RET_HEADS // CHUNK_SPLIT
    n_blocks = DEC_BATCH // SAMPLE_BLOCK

    def state_copy(step):
        slot = step % STATE_RING
        row0 = (step % n_blocks) * SAMPLE_BLOCK
        if not isinstance(step, int):
            row0 = pl.multiple_of(row0, SAMPLE_BLOCK)
        return pltpu.make_async_copy(st_hbm.at[layer, pl.ds(row0, SAMPLE_BLOCK), step // n_blocks],
                                     ring.at[slot], ring_sems.at[slot])

    @pl.when(s < n_main)
    def _():
        part = lax.rem(s, CHUNK_SPLIT)
        c = lax.rem(s // CHUNK_SPLIT, n_chunks)

        @pl.when(s == 0)
        def _():
            for step in range(STATE_RING - 1):
                state_copy(step).start()
            ysm_scr[...] = jnp.zeros(ysm_scr.shape, F32)
            for h in range(RET_HEADS):
                ysm_scr[h, 0:N_META, :] = ymeta_ref[0:N_META, h * RET_DV:(h + 1) * RET_DV].astype(F32)

        @pl.when(s + STATE_RING - 1 < n_main)
        def _():
            state_copy(s + STATE_RING - 1).start()

        @pl.when(jnp.logical_and(c == 0, part == 0))
        def _():
            s_scr[...] = s0_ref[...]

        for p in range(CHUNK_SPLIT):
            @pl.when(part == p)
            def _(p=p):
                _chunk_heads(range(p * heads_per_step, (p + 1) * heads_per_step),
                             lg_ref, q_ref, k_ref, v_ref, g_ref, s_scr, y_ref, chunk, valid)

        @pl.when(jnp.logical_and(c == n_chunks - 1, part == CHUNK_SPLIT - 1))
        def _():
            sout_ref[0] = s_scr[...]

        state_copy(s).wait()
        _sample_head(s // n_blocks, lax.rem(s, n_blocks), lg_ref, qt_ref, kt_ref, vs_ref, gs_ref,
                     ring.at[lax.rem(s, STATE_RING)], sto_ref, ysm_scr)

    @pl.when(s >= n_main)
    def _():
        rows = pl.ds(pl.multiple_of((s - n_main) * chunk, chunk), chunk)
        for h in range(RET_HEADS):
            y_ref[:, h * RET_DV:(h + 1) * RET_DV] = ysm_scr[h, rows, :].astype(BF16)


def _retention(log_gamma, qkvg, s0, sample, *, n_seq, n_chunks, chunk, valid):
    k_col = 1
    v_col = 2 * D_MODEL // RET_VDIM
    fused = sample is not None
    if not fused:
        n_steps, n_rows = n_seq * n_chunks, n_seq * n_chunks * chunk
        blk = lambda s: s
        y_blk = lambda s: s
        seq = lambda s: s // n_chunks
    else:
        y_meta, qt, kt, vs, gs, state_ret, prev_out, layer = sample
        n_small = SMALL_ROWS // chunk
        n_main = n_seq * n_chunks * CHUNK_SPLIT
        assert n_main == RET_HEADS * (DEC_BATCH // SAMPLE_BLOCK)
        n_steps, n_rows = n_main + n_small, N_ROWS
        blk = lambda s: n_small + jnp.minimum(s, n_main - 1) // CHUNK_SPLIT
        y_blk = lambda s: jnp.where(s < n_main, n_small + s // CHUNK_SPLIT, s - n_main)
        seq = lambda s: jnp.minimum(s, n_main - 1) // (CHUNK_SPLIT * n_chunks)
    in_specs = [
        pl.BlockSpec(memory_space=pltpu.SMEM),
        pl.BlockSpec((chunk, D_MODEL), lambda s: (blk(s), 0)),
        pl.BlockSpec((chunk, D_MODEL), lambda s: (blk(s), k_col)),
        pl.BlockSpec((chunk, RET_VDIM), lambda s: (blk(s), v_col)),
        pl.BlockSpec((chunk, RET_VDIM), lambda s: (blk(s), v_col + 1)),
        pl.BlockSpec((RET_HEADS, RET_DK, RET_DV), lambda s: (0, 0, 0)),
    ]
    args = [log_gamma, qkvg, qkvg, qkvg, qkvg, s0]
    out_specs = [
        pl.BlockSpec((chunk, RET_VDIM), lambda s: (y_blk(s), 0)),
        pl.BlockSpec((1, RET_HEADS, RET_DK, RET_DV), lambda s: (seq(s), 0, 0, 0)),
    ]
    out_shape = [
        jax.ShapeDtypeStruct((n_rows, RET_VDIM), BF16),
        jax.ShapeDtypeStruct((n_seq, RET_HEADS, RET_DK, RET_DV), F32),
    ]
    scratch = [pltpu.VMEM((RET_HEADS, RET_DK, RET_DV), F32)]
    aliases = {}
    if fused:
        n_blocks = DEC_BATCH // SAMPLE_BLOCK
        head = lambda s: jnp.minimum(s, n_main - 1) // n_blocks
        sblk = lambda s: lax.rem(jnp.minimum(s, n_main - 1), n_blocks)
        state_spec = pl.BlockSpec((1, SAMPLE_BLOCK, 1, RET_DK, RET_DV),
                                  lambda s: (layer, sblk(s), head(s), 0, 0))
        t_spec = pl.BlockSpec((1, RET_DK, DEC_BATCH), lambda s: (head(s), 0, 0))
        vg_spec = pl.BlockSpec((SAMPLE_BLOCK, RET_DV), lambda s: (sblk(s), head(s)))
        in_specs += [pl.BlockSpec((chunk, RET_VDIM), lambda s: (0, 0)), t_spec, t_spec, vg_spec, vg_spec,
                     pl.BlockSpec(memory_space=pl.ANY)]
        args += [y_meta, qt, kt, vs, gs, state_ret]
        if prev_out is not None:
            aliases = {len(args): 2}
            in_specs.append(pl.BlockSpec(memory_space=pl.ANY))
            args.append(prev_out)
        out_specs.append(state_spec)
        out_shape.append(jax.ShapeDtypeStruct(state_ret.shape, F32))
        scratch += [
            pltpu.VMEM((RET_HEADS, SMALL_ROWS, RET_DV), F32),
            pltpu.VMEM((STATE_RING, SAMPLE_BLOCK, RET_DK, RET_DV), F32),
            pltpu.SemaphoreType.DMA((STATE_RING,)),
        ]
    return pl.pallas_call(
        functools.partial(_ret_kernel, chunk=chunk, valid=valid, n_chunks=n_chunks, n_seq=n_seq,
                          fused=fused, aliased=bool(aliases), layer=layer if fused else None),
        grid=(n_steps,),
        in_specs=in_specs,
        out_specs=out_specs,
        out_shape=out_shape,
        scratch_shapes=scratch,
        input_output_aliases=aliases,
        compiler_params=_params("arbitrary"),
        name="retention",
    )(*args)


def _outproj_kernel(x_ref, y_ref, w_ref, o_ref):
    for n in range(D_MODEL // OUT_TILE):
        sl = slice(n * OUT_TILE, (n + 1) * OUT_TILE)
        o_ref[:, sl] = x_ref[:, sl] + jnp.dot(y_ref[...], w_ref[0, :, sl].astype(BF16),
                                              preferred_element_type=F32)


def _ret_outproj(x, y, w_out, layer):
    x_spec = pl.BlockSpec((OUTPROJ_ROW_TILE, D_MODEL), lambda i: (i, 0))
    return pl.pallas_call(
        _outproj_kernel,
        grid=(N_ROWS // OUTPROJ_ROW_TILE,),
        in_specs=[
            x_spec,
            pl.BlockSpec((OUTPROJ_ROW_TILE, RET_VDIM), lambda i: (i, 0)),
            pl.BlockSpec((1, RET_VDIM, D_MODEL), lambda i: (layer, 0, 0), pipeline_mode=pl.Buffered(1)),
        ],
        out_specs=x_spec,
        out_shape=jax.ShapeDtypeStruct((N_ROWS, D_MODEL), F32),
        compiler_params=_params("arbitrary"),
        name="ret_outproj",
    )(x, y, w_out)


def _rope_tables():
    pos = np.zeros((N_ROWS,), np.int32)
    pos[:N_META] = np.arange(N_META)
    pos[N_META:N_META + DEC_BATCH] = PAST_LEN
    pos[SMALL_ROWS:] = np.tile(N_META + np.arange(SEQ), BATCH)
    half = RET_DK // 2
    inv = ROPE_BASE ** (-jnp.arange(half, dtype=F32) / half)
    ang = jnp.asarray(pos).astype(F32)[:, None] * inv[None, :]
    return jnp.cos(ang), jnp.sin(ang)


def kernel(x_prompt, x_sample, state_pool, state_ret, meta_tokens, pool_norm, pool_w, pool_scale,
           ret_norm, ret_w_in, ret_w_out, mlp_norm, mlp_w_up, mlp_w_down, final_norm):
    assert x_prompt.shape == (BATCH, SEQ, D_MODEL) and x_sample.shape == (DEC_BATCH, 1, D_MODEL)
    pad_rows = SMALL_ROWS - N_META - DEC_BATCH
    x_small = jnp.concatenate([
        meta_tokens.astype(F32),
        x_sample.reshape(DEC_BATCH, D_MODEL),
        jnp.zeros((pad_rows, D_MODEL), F32),
    ], axis=0)
    x = None
    log_gamma = jnp.log1p(-jnp.exp2(-5.0 - jnp.arange(RET_HEADS, dtype=F32)))
    cos, sin = _rope_tables()
    state_slots = state_pool.transpose(0, 2, 1, 3)
    carry_sums = _pool_carry_sums(state_slots)

    pool_p, pool_s, ret_p = [], [], []
    ret_s = None
    for i in range(DEPTH):
        j = i // 2
        if i % 2 == 0:
            carry_ext = jnp.pad(carry_sums[j], ((N_META, pad_rows), (0, 0)))
            if x is None:
                xs, xr, block0 = x_small, x_prompt.reshape(BATCH * SEQ, D_MODEL), 0
            else:
                xs, xr, block0 = x, x, SMALL_ROWS // POOL_TILE
            x, u_small, u_tail = _pool_layer(xs, xr, block0, carry_ext, pool_norm[j], pool_scale[j], pool_w, j)
            pool_p.append(u_tail[:, 1:])
            pool_s.append(u_small)
        else:
            qkvg = _ret_proj(x, ret_norm[j], ret_w_in, j, cos, sin)
            zero_state = jnp.zeros((RET_HEADS, RET_DK, RET_DV), F32)
            y_meta, s_meta = _retention(log_gamma, qkvg, zero_state, None,
                                        n_seq=1, n_chunks=1, chunk=RET_CHUNK, valid=N_META)
            qkvg_s = qkvg[N_META:N_META + DEC_BATCH]
            qt = qkvg_s[:, :D_MODEL].reshape(DEC_BATCH, RET_HEADS, RET_DK).transpose(1, 2, 0)
            kt = qkvg_s[:, D_MODEL:2 * D_MODEL].reshape(DEC_BATCH, RET_HEADS, RET_DK).transpose(1, 2, 0)
            sample = (y_meta, qt, kt, qkvg_s[:, 2 * D_MODEL:2 * D_MODEL + RET_VDIM].astype(F32),
                      qkvg_s[:, 2 * D_MODEL + RET_VDIM:].astype(F32), state_ret, ret_s, j)
            y, s_real, ret_s = _retention(log_gamma, qkvg, s_meta[0], sample, n_seq=BATCH,
                                          n_chunks=SEQ // RET_CHUNK, chunk=RET_CHUNK, valid=RET_CHUNK)
            ret_p.append(s_real)
            x = _ret_outproj(x, y, ret_w_out, j)
        x = _mlp_layer(x, mlp_norm[i], mlp_w_up, mlp_w_down, i, final_norm, i == DEPTH - 1)

    y_prompt = x[0].reshape(BATCH, SEQ, D_MODEL)
    y_sample = x[1].reshape(DEC_BATCH, 1, D_MODEL)
    new_pool_sample = _pool_sample_states(state_slots, *pool_s).transpose(0, 2, 1, 3)
    return (y_prompt, y_sample, jnp.stack(pool_p, axis=0), new_pool_sample,
            jnp.stack(ret_p, axis=0), ret_s)
```

```python
import functools

import numpy as np
import jax
import jax.numpy as jnp
from jax import lax
from jax.experimental import pallas as pl
from jax.experimental.pallas import tpu as pltpu

D_MODEL = 2048
BATCH = 4
SEQ = 2048
DEPTH = 4
DEC_BATCH = 128
PAST_LEN = 16384
N_META = 16
POOL_WINDOWS = (2, 4, 8, 16)
POOL_GROUP_DIM = D_MODEL // len(POOL_WINDOWS)
POOL_BUF = max(POOL_WINDOWS) - 1
RET_HEADS = 8
RET_DK = D_MODEL // RET_HEADS
RET_DV = 2 * RET_DK
RET_VDIM = RET_HEADS * RET_DV
RET_IN = 2 * D_MODEL + 2 * RET_VDIM
RET_CHUNK = 128
ROPE_BASE = 10000.0
D_FF = 4 * D_MODEL
EPS = 1e-6

SMALL_ROWS = 256
N_ROWS = SMALL_ROWS + BATCH * SEQ
ROW_TILE = N_ROWS // 8
POOL_TILE = 256
POOL_TILES_PER_BATCH = SEQ // POOL_TILE
POOL_STATE_ROWS = 32
FF_TILE = 1024
FF_CHUNK = 512
PROJ_TILE = 2048
OUT_TILE = 512
OUTPROJ_ROW_TILE = N_ROWS // 16
SAMPLE_BLOCK = 8
CHUNK_SPLIT = 2
WRAP_REAL_ROWS = ROW_TILE - SMALL_ROWS
POOL_RING = 3
STATE_RING = 3
VMEM_CAPACITY = 64 * 1024 * 1024
VMEM_LIMIT = 60 * 1024 * 1024

F32 = jnp.float32
BF16 = jnp.bfloat16


def _rmsnorm(x, g):
    return x * lax.rsqrt(jnp.mean(x * x, axis=-1, keepdims=True) + EPS) * g


def _params(*semantics):
    return pltpu.CompilerParams(dimension_semantics=semantics, vmem_limit_bytes=VMEM_LIMIT)


def _poolsum_kernel(s_ref, o_ref):
    for gi, w in enumerate(POOL_WINDOWS):
        sl = slice(gi * POOL_GROUP_DIM, (gi + 1) * POOL_GROUP_DIM)
        acc = s_ref[0, POOL_BUF - 1, :, sl]
        for r in range(POOL_BUF - (w - 1), POOL_BUF - 1):
            acc = acc + s_ref[0, r, :, sl]
        o_ref[0, :, sl] = acc


def _pool_carry_sums(state_slots):
    n_layers = state_slots.shape[0]
    n_rows = POOL_STATE_ROWS
    return pl.pallas_call(
        _poolsum_kernel,
        grid=(n_layers, DEC_BATCH // n_rows),
        in_specs=[pl.BlockSpec((1, POOL_BUF, n_rows, D_MODEL), lambda j, n: (j, 0, n, 0))],
        out_specs=pl.BlockSpec((1, n_rows, D_MODEL), lambda j, n: (j, n, 0)),
        out_shape=jax.ShapeDtypeStruct((n_layers, DEC_BATCH, D_MODEL), F32),
        compiler_params=_params("arbitrary", "arbitrary"),
        name="pool_carry_sums",
    )(state_slots)


def _pool_state_kernel(s_ref, u0_ref, u1_ref, o_ref):
    for r in range(POOL_BUF - 1):
        o_ref[0, r] = s_ref[0, r + 1]
    rows = pl.ds(pl.multiple_of(N_META + pl.program_id(1) * POOL_STATE_ROWS, 8), POOL_STATE_ROWS)

    @pl.when(pl.program_id(0) == 0)
    def _():
        o_ref[0, POOL_BUF - 1] = u0_ref[rows, :]

    @pl.when(pl.program_id(0) == 1)
    def _():
        o_ref[0, POOL_BUF - 1] = u1_ref[rows, :]


def _pool_sample_states(state_slots, u_small0, u_small1):
    assert state_slots.shape[0] == 2
    u_spec = pl.BlockSpec((SMALL_ROWS, D_MODEL), lambda j, n: (0, 0))
    state_spec = pl.BlockSpec((1, POOL_BUF, POOL_STATE_ROWS, D_MODEL), lambda j, n: (j, 0, n, 0))
    return pl.pallas_call(
        _pool_state_kernel,
        grid=(state_slots.shape[0], DEC_BATCH // POOL_STATE_ROWS),
        in_specs=[state_spec, u_spec, u_spec],
        out_specs=state_spec,
        out_shape=jax.ShapeDtypeStruct(state_slots.shape, F32),
        compiler_params=_params("arbitrary", "arbitrary"),
        name="pool_sample_states",
    )(state_slots, u_small0, u_small1)


def _pool_kernel(xs_hbm, xr_hbm, carry_ref, g_ref, sc_ref, w_ref, o_ref, usmall_ref, utail_ref,
                 prev_scr, meta_scr, w_scr, ring, ring_sems, *, real_block0):
    t = pl.program_id(0)
    is_small = t == 0
    chunk = lax.rem(jnp.maximum(t - 1, 0), POOL_TILES_PER_BATCH)

    def small_copy():
        return pltpu.make_async_copy(xs_hbm.at[pl.ds(0, POOL_TILE), :], ring.at[0], ring_sems.at[0])

    def real_copy(tile):
        slot = tile % POOL_RING
        row0 = (real_block0 + tile - 1) * POOL_TILE
        if not isinstance(tile, int):
            row0 = pl.multiple_of(row0, POOL_TILE)
        return pltpu.make_async_copy(xr_hbm.at[pl.ds(row0, POOL_TILE), :], ring.at[slot], ring_sems.at[slot])

    @pl.when(is_small)
    def _():
        small_copy().start()
        for tile in range(1, POOL_RING - 1):
            real_copy(tile).start()
        small_copy().wait()

    @pl.when(t + POOL_RING - 1 < pl.num_programs(0))
    def _():
        real_copy(t + POOL_RING - 1).start()

    @pl.when(t > 0)
    def _():
        real_copy(t).wait()

    x = ring[lax.rem(t, POOL_RING)]
    u = _rmsnorm(x, g_ref[...])

    @pl.when(is_small)
    def _():
        prev_scr[...] = jnp.zeros((16, D_MODEL), F32)
        meta_scr[...] = u[0:16, :]
        usmall_ref[...] = u
        w_scr[...] = w_ref[0].astype(BF16)

    @pl.when(jnp.logical_and(t > 0, chunk == 0))
    def _():
        prev_scr[...] = meta_scr[...]

    ext = jnp.concatenate([prev_scr[...], u], axis=0)
    row = lax.broadcasted_iota(jnp.int32, (POOL_TILE, 1), 0)
    meta_row = jnp.logical_and(is_small, row < N_META)
    sample_row = jnp.logical_and(is_small, row >= N_META)
    for gi, w in enumerate(POOL_WINDOWS):
        sl = slice(gi * POOL_GROUP_DIM, (gi + 1) * POOL_GROUP_DIM)
        s = ext[:, sl]
        shift = 1
        while shift < w:
            s = s + pltpu.roll(s, shift, 0)
            shift *= 2
        ug = u[:, sl]
        ws = jnp.where(sample_row, carry_ref[:, sl] + ug, s[16:, :])
        inv_cnt = 1.0 / jnp.where(meta_row, jnp.minimum(w, row + 1), w).astype(F32)
        d = (ws * inv_cnt - ug).astype(BF16)
        y = jnp.dot(d, w_scr[gi], preferred_element_type=F32)
        o_ref[:, sl] = x[:, sl] + y * sc_ref[:, sl]

    tail = u[POOL_TILE - 16:, :]
    prev_scr[...] = tail

    @pl.when(jnp.logical_and(t > 0, chunk == POOL_TILES_PER_BATCH - 1))
    def _():
        utail_ref[0] = tail


def _pool_layer(x_small, x_real, real_block0, carry_ext, norm_g, scale, pool_w, layer):
    n_tiles = N_ROWS // POOL_TILE
    row_spec = pl.BlockSpec((POOL_TILE, D_MODEL), lambda t: (t, 0))
    first_spec = pl.BlockSpec((POOL_TILE, D_MODEL), lambda t: (0, 0))
    vec_spec = pl.BlockSpec((1, D_MODEL), lambda t: (0, 0))
    n_groups = len(POOL_WINDOWS)
    return pl.pallas_call(
        functools.partial(_pool_kernel, real_block0=real_block0),
        grid=(n_tiles,),
        in_specs=[
            pl.BlockSpec(memory_space=pl.ANY),
            pl.BlockSpec(memory_space=pl.ANY),
            first_spec,
            vec_spec,
            vec_spec,
            pl.BlockSpec((1, n_groups, POOL_GROUP_DIM, POOL_GROUP_DIM), lambda t: (layer, 0, 0, 0)),
        ],
        out_specs=[
            row_spec,
            first_spec,
            pl.BlockSpec((1, 16, D_MODEL), lambda t: (jnp.maximum(t - 1, 0) // POOL_TILES_PER_BATCH, 0, 0)),
        ],
        out_shape=[
            jax.ShapeDtypeStruct((N_ROWS, D_MODEL), F32),
            jax.ShapeDtypeStruct((SMALL_ROWS, D_MODEL), F32),
            jax.ShapeDtypeStruct((BATCH, 16, D_MODEL), F32),
        ],
        scratch_shapes=[
            pltpu.VMEM((16, D_MODEL), F32),
            pltpu.VMEM((16, D_MODEL), F32),
            pltpu.VMEM((n_groups, POOL_GROUP_DIM, POOL_GROUP_DIM), BF16),
            pltpu.VMEM((POOL_RING, POOL_TILE, D_MODEL), F32),
            pltpu.SemaphoreType.DMA((POOL_RING,)),
        ],
        compiler_params=_params("arbitrary"),
        name="pool_layer",
    )(x_small, x_real, carry_ext, norm_g.reshape(1, D_MODEL), scale.reshape(1, D_MODEL), pool_w)


def _consume_row_tile(x_hbm, x_scr, x_sems, consume, wrap=False):
    i = pl.program_id(0)
    j = pl.program_id(1)
    last = pl.num_programs(0) - 1

    def whole_tile_copy(tile):
        row0 = SMALL_ROWS + tile * ROW_TILE if wrap else tile * ROW_TILE
        rows = pl.ds(pl.multiple_of(row0, 8), ROW_TILE)
        return [pltpu.make_async_copy(x_hbm.at[rows, :], x_scr, x_sems.at[0])]

    def wrapped_tile_copy():
        return [
            pltpu.make_async_copy(x_hbm.at[pl.ds(N_ROWS - WRAP_REAL_ROWS, WRAP_REAL_ROWS), :],
                                  x_scr.at[pl.ds(0, WRAP_REAL_ROWS), :], x_sems.at[0]),
            pltpu.make_async_copy(x_hbm.at[pl.ds(0, SMALL_ROWS), :],
                                  x_scr.at[pl.ds(WRAP_REAL_ROWS, SMALL_ROWS), :], x_sems.at[1]),
        ]

    def for_tile(tile, action):
        if not wrap:
            for c in whole_tile_copy(tile):
                action(c)
            return

        @pl.when(tile < last)
        def _():
            for c in whole_tile_copy(tile):
                action(c)

        @pl.when(tile == last)
        def _():
            for c in wrapped_tile_copy():
                action(c)

    @pl.when(jnp.logical_and(i == 0, j == 0))
    def _():
        for_tile(i, lambda c: c.start())

    @pl.when(j == 0)
    def _():
        for_tile(i, lambda c: c.wait())
        consume()

    @pl.when(jnp.logical_and(j == 1, i < last))
    def _():
        for_tile(i + 1, lambda c: c.start())


def _mlp_kernel(*refs, final):
    if final:
        x_hbm, g_ref, wu_ref, wd_ref, gf_ref, o_ref, ys_ref, x_scr, u_scr, x_sems = refs
    else:
        x_hbm, g_ref, wu_ref, wd_ref, gf_ref, o_ref, x_scr, u_scr, x_sems = refs
    j = pl.program_id(1)

    def start_tile():
        x = x_scr[...]
        u_scr[...] = _rmsnorm(x, g_ref[...]).astype(BF16)
        o_ref[...] = x

    _consume_row_tile(x_hbm, x_scr, x_sems, start_tile, wrap=final)

    for c in range(FF_TILE // FF_CHUNK):
        ff = slice(c * FF_CHUNK, (c + 1) * FF_CHUNK)
        h = jnp.dot(u_scr[...], wu_ref[0, :, ff].astype(BF16), preferred_element_type=F32)
        h = jnp.square(jnp.maximum(h, 0.0)).astype(BF16)
        for n in range(D_MODEL // OUT_TILE):
            sl = slice(n * OUT_TILE, (n + 1) * OUT_TILE)
            o_ref[:, sl] += jnp.dot(h, wd_ref[0, ff, sl].astype(BF16), preferred_element_type=F32)

    if final:
        @pl.when(j == pl.num_programs(1) - 1)
        def _():
            o_ref[...] = _rmsnorm(o_ref[...], gf_ref[...])

            @pl.when(pl.program_id(0) == pl.num_programs(0) - 1)
            def _():
                rows = pl.ds(WRAP_REAL_ROWS + N_META, DEC_BATCH)
                copy = pltpu.make_async_copy(o_ref.at[rows, :], ys_ref, x_sems.at[2])
                copy.start()
                copy.wait()


def _mlp_layer(x, norm_g, w_up, w_down, layer, final_g, final):
    row_spec = pl.BlockSpec((ROW_TILE, D_MODEL), lambda i, j: (i, 0))
    vec_spec = pl.BlockSpec((1, D_MODEL), lambda i, j: (0, 0))
    if final:
        out_specs = [row_spec, pl.BlockSpec(memory_space=pl.ANY)]
        out_shape = [jax.ShapeDtypeStruct((BATCH * SEQ, D_MODEL), F32),
                     jax.ShapeDtypeStruct((DEC_BATCH, D_MODEL), F32)]
    else:
        out_specs, out_shape = row_spec, jax.ShapeDtypeStruct((N_ROWS, D_MODEL), F32)
    return pl.pallas_call(
        functools.partial(_mlp_kernel, final=final),
        grid=(N_ROWS // ROW_TILE, D_FF // FF_TILE),
        in_specs=[
            pl.BlockSpec(memory_space=pl.ANY),
            vec_spec,
            pl.BlockSpec((1, D_MODEL, FF_TILE), lambda i, j: (layer, 0, j)),
            pl.BlockSpec((1, FF_TILE, D_MODEL), lambda i, j: (layer, j, 0)),
            vec_spec,
        ],
        out_specs=out_specs,
        out_shape=out_shape,
        scratch_shapes=[
            pltpu.VMEM((ROW_TILE, D_MODEL), F32),
            pltpu.VMEM((ROW_TILE, D_MODEL), BF16),
            pltpu.SemaphoreType.DMA((3,)),
        ],
        compiler_params=pltpu.CompilerParams(dimension_semantics=("arbitrary", "arbitrary"),
                                             vmem_limit_bytes=VMEM_CAPACITY),
        name="mlp_layer",
    )(x, norm_g.reshape(1, D_MODEL), w_up, w_down, final_g.reshape(1, D_MODEL))


def _proj_kernel(x_hbm, g_ref, w_ref, cos_ref, sin_ref, o_ref, x_scr, u_scr, x_sems):
    j = pl.program_id(1)
    n_rope_tiles = 2 * D_MODEL // PROJ_TILE

    def start_tile():
        u_scr[...] = _rmsnorm(x_scr[...], g_ref[...]).astype(BF16)

    _consume_row_tile(x_hbm, x_scr, x_sems, start_tile)

    is_rope = j < n_rope_tiles
    k_scale = jnp.where(j >= n_rope_tiles // 2, RET_DK ** -0.5, 1.0).astype(F32)
    cos = jnp.where(is_rope, cos_ref[...] * k_scale, 1.0)
    sin = jnp.where(is_rope, sin_ref[...] * k_scale, 0.0)
    half = RET_DK // 2
    for hh in range(PROJ_TILE // RET_DK):
        acc = jnp.dot(u_scr[...], w_ref[0, :, hh * RET_DK:(hh + 1) * RET_DK].astype(BF16),
                      preferred_element_type=F32)
        x1 = acc[:, :half]
        x2 = acc[:, half:]
        o_ref[:, hh * RET_DK:hh * RET_DK + half] = (x1 * cos - x2 * sin).astype(BF16)
        o_ref[:, hh * RET_DK + half:(hh + 1) * RET_DK] = (x1 * sin + x2 * cos).astype(BF16)


def _ret_proj(x, norm_g, w_in, layer, cos, sin):
    vec_spec = pl.BlockSpec((1, D_MODEL), lambda i, j: (0, 0))
    rope_spec = pl.BlockSpec((ROW_TILE, RET_DK // 2), lambda i, j: (i, 0))
    return pl.pallas_call(
        _proj_kernel,
        grid=(N_ROWS // ROW_TILE, RET_IN // PROJ_TILE),
        in_specs=[
            pl.BlockSpec(memory_space=pl.ANY),
            vec_spec,
            pl.BlockSpec((1, D_MODEL, PROJ_TILE), lambda i, j: (layer, 0, j)),
            rope_spec,
            rope_spec,
        ],
        out_specs=pl.BlockSpec((ROW_TILE, PROJ_TILE), lambda i, j: (i, j)),
        out_shape=jax.ShapeDtypeStruct((N_ROWS, RET_IN), BF16),
        scratch_shapes=[
            pltpu.VMEM((ROW_TILE, D_MODEL), F32),
            pltpu.VMEM((ROW_TILE, D_MODEL), BF16),
            pltpu.SemaphoreType.DMA((1,)),
        ],
        compiler_params=pltpu.CompilerParams(dimension_semantics=("arbitrary", "arbitrary"),
                                             vmem_limit_bytes=VMEM_CAPACITY),
        name="ret_proj",
    )(x, norm_g.reshape(1, D_MODEL), w_in, cos, sin)


def _head_norm_gate(o, g):
    mu = jnp.mean(o, axis=-1, keepdims=True)
    oc = o - mu
    var = jnp.mean(oc * oc, axis=-1, keepdims=True)
    return oc * lax.rsqrt(var + EPS) * (g * jax.nn.sigmoid(g))


def _chunk_heads(heads, lg_ref, q_ref, k_ref, v_ref, g_ref, s_scr, y_ref, chunk, valid):
    row = lax.broadcasted_iota(jnp.int32, (chunk, 1), 0).astype(F32)
    col = lax.broadcasted_iota(jnp.int32, (1, chunk), 1).astype(F32)
    diff = row - col
    for h in heads:
        lg = lg_ref[h]
        qh = q_ref[:, h * RET_DK:(h + 1) * RET_DK]
        kh = k_ref[:, h * RET_DK:(h + 1) * RET_DK]
        vh = v_ref[:, h * RET_DV:(h + 1) * RET_DV]
        decay = jnp.where(diff >= 0, jnp.exp(jnp.maximum(diff, 0.0) * lg), 0.0)
        scores = lax.dot_general(qh, kh, (((1,), (1,)), ((), ())), preferred_element_type=F32) * decay
        inner = jnp.dot(scores.astype(BF16), vh, preferred_element_type=F32)
        st = s_scr[h]
        cross = jnp.dot(qh, st.astype(BF16), preferred_element_type=F32) * jnp.exp((row + 1.0) * lg)
        o = inner + cross
        kd = kh.astype(F32) * jnp.exp((valid - 1.0 - row) * lg)
        if valid < chunk:
            kd = jnp.where(row < valid, kd, 0.0)
        kd = kd.astype(BF16)
        s_dec = jnp.exp(jnp.full((1, RET_DV), valid, F32) * lg)
        s_scr[h] = st * s_dec + lax.dot_general(kd, vh, (((0,), (0,)), ((), ())),
                                               preferred_element_type=F32)
        gh = g_ref[:, h * RET_DV:(h + 1) * RET_DV].astype(F32)
        y_ref[:, h * RET_DV:(h + 1) * RET_DV] = _head_norm_gate(o, gh).astype(BF16)


def _sample_head(h, nb, lg_ref, qt_ref, kt_ref, v_ref, g_ref, s_ref, so_ref, ysm_scr):
    lg = lg_ref[h]
    gamma = jnp.exp(jnp.full((1, 128), 1.0, F32) * lg)
    sample = lax.broadcasted_iota(jnp.int32, (DEC_BATCH, 128), 0)
    row8 = lax.broadcasted_iota(jnp.int32, (SAMPLE_BLOCK, 1), 0)
    qt = qt_ref[0]
    kt = kt_ref[0]
    o_all = jnp.zeros((SAMPLE_BLOCK, RET_DV), F32)
    for n in range(SAMPLE_BLOCK):
        onehot = (sample == nb * SAMPLE_BLOCK + n).astype(BF16)
        qb = jnp.dot(qt, onehot, preferred_element_type=F32)
        kb = jnp.dot(kt, onehot, preferred_element_type=F32)
        v_row = v_ref[n:n + 1, :]
        parts = []
        for vt in range(RET_DV // 128):
            sl = slice(vt * 128, (vt + 1) * 128)
            s_new = s_ref[n, :, sl] * gamma + kb * v_row[:, sl]
            so_ref[0, n, 0, :, sl] = s_new
            parts.append(jnp.sum(s_new * qb, axis=0, keepdims=True))
        o_n = jnp.concatenate(parts, axis=1)
        o_all = jnp.where(row8 == n, o_n, o_all)
    rows = pl.ds(pl.multiple_of(N_META + nb * SAMPLE_BLOCK, SAMPLE_BLOCK), SAMPLE_BLOCK)
    ysm_scr[h, rows, :] = _head_norm_gate(o_all, g_ref[...])


def _ret_kernel(*refs, chunk, valid, n_chunks, n_seq, fused, aliased, layer):
    if not fused:
        lg_ref, q_ref, k_ref, v_ref, g_ref, s0_ref, y_ref, sout_ref, s_scr = refs
        s_scr[...] = s0_ref[...]
        _chunk_heads(range(RET_HEADS), lg_ref, q_ref, k_ref, v_ref, g_ref, s_scr, y_ref, chunk, valid)
        sout_ref[0] = s_scr[...]
        return

    (lg_ref, q_ref, k_ref, v_ref, g_ref, s0_ref, ymeta_ref, qt_ref, kt_ref, vs_ref, gs_ref,
     st_hbm) = refs[:12]
    y_ref, sout_ref, sto_ref, s_scr, ysm_scr, ring, ring_sems = refs[12 + (1 if aliased else 0):]
    s = pl.program_id(0)
    n_main = n_seq * n_chunks * CHUNK_SPLIT
    heads_per_step = RET_HEADS // CHUNK_SPLIT
    n_blocks = DEC_BATCH // SAMPLE_BLOCK

    def state_copy(step):
        slot = step % STATE_RING
        row0 = (step % n_blocks) * SAMPLE_BLOCK
        if not isinstance(step, int):
            row0 = pl.multiple_of(row0, SAMPLE_BLOCK)
        return pltpu.make_async_copy(st_hbm.at[layer, pl.ds(row0, SAMPLE_BLOCK), step // n_blocks],
                                     ring.at[slot], ring_sems.at[slot])

    @pl.when(s < n_main)
    def _():
        part = lax.rem(s, CHUNK_SPLIT)
        c = lax.rem(s // CHUNK_SPLIT, n_chunks)

        @pl.when(s == 0)
        def _():
            for step in range(STATE_RING - 1):
                state_copy(step).start()
            ysm_scr[...] = jnp.zeros(ysm_scr.shape, F32)
            for h in range(RET_HEADS):
                ysm_scr[h, 0:N_META, :] = ymeta_ref[0:N_META, h * RET_DV:(h + 1) * RET_DV].astype(F32)

        @pl.when(s + STATE_RING - 1 < n_main)
        def _():
            state_copy(s + STATE_RING - 1).start()

        @pl.when(jnp.logical_and(c == 0, part == 0))
        def _():
            s_scr[...] = s0_ref[...]

        for p in range(CHUNK_SPLIT):
            @pl.when(part == p)
            def _(p=p):
                _chunk_heads(range(p * heads_per_step, (p + 1) * heads_per_step),
                             lg_ref, q_ref, k_ref, v_ref, g_ref, s_scr, y_ref, chunk, valid)

        @pl.when(jnp.logical_and(c == n_chunks - 1, part == CHUNK_SPLIT - 1))
        def _():
            sout_ref[0] = s_scr[...]

        state_copy(s).wait()
        _sample_head(s // n_blocks, lax.rem(s, n_blocks), lg_ref, qt_ref, kt_ref, vs_ref, gs_ref,
                     ring.at[lax.rem(s, STATE_RING)], sto_ref, ysm_scr)

    @pl.when(s >= n_main)
    def _():
        rows = pl.ds(pl.multiple_of((s - n_main) * chunk, chunk), chunk)
        for h in range(RET_HEADS):
            y_ref[:, h * RET_DV:(h + 1) * RET_DV] = ysm_scr[h, rows, :].astype(BF16)


def _retention(log_gamma, qkvg, s0, sample, *, n_seq, n_chunks, chunk, valid):
    k_col = 1
    v_col = 2 * D_MODEL // RET_VDIM
    fused = sample is not None
    if not fused:
        n_steps, n_rows = n_seq * n_chunks, n_seq * n_chunks * chunk
        blk = lambda s: s
        y_blk = lambda s: s
        seq = lambda s: s // n_chunks
    else:
        y_meta, qt, kt, vs, gs, state_ret, prev_out, layer = sample
        n_small = SMALL_ROWS // chunk
        n_main = n_seq * n_chunks * CHUNK_SPLIT
        assert n_main == RET_HEADS * (DEC_BATCH // SAMPLE_BLOCK)
        n_steps, n_rows = n_main + n_small, N_ROWS
        blk = lambda s: n_small + jnp.minimum(s, n_main - 1) // CHUNK_SPLIT
        y_blk = lambda s: jnp.where(s < n_main, n_small + s // CHUNK_SPLIT, s - n_main)
        seq = lambda s: jnp.minimum(s, n_main - 1) // (CHUNK_SPLIT * n_chunks)
    in_specs = [
        pl.BlockSpec(memory_space=pltpu.SMEM),
        pl.BlockSpec((chunk, D_MODEL), lambda s: (blk(s), 0)),
        pl.BlockSpec((chunk, D_MODEL), lambda s: (blk(s), k_col)),
        pl.BlockSpec((chunk, RET_VDIM), lambda s: (blk(s), v_col)),
        pl.BlockSpec((chunk, RET_VDIM), lambda s: (blk(s), v_col + 1)),
        pl.BlockSpec((RET_HEADS, RET_DK, RET_DV), lambda s: (0, 0, 0)),
    ]
    args = [log_gamma, qkvg, qkvg, qkvg, qkvg, s0]
    out_specs = [
        pl.BlockSpec((chunk, RET_VDIM), lambda s: (y_blk(s), 0)),
        pl.BlockSpec((1, RET_HEADS, RET_DK, RET_DV), lambda s: (seq(s), 0, 0, 0)),
    ]
    out_shape = [
        jax.ShapeDtypeStruct((n_rows, RET_VDIM), BF16),
        jax.ShapeDtypeStruct((n_seq, RET_HEADS, RET_DK, RET_DV), F32),
    ]
    scratch = [pltpu.VMEM((RET_HEADS, RET_DK, RET_DV), F32)]
    aliases = {}
    if fused:
        n_blocks = DEC_BATCH // SAMPLE_BLOCK
        head = lambda s: jnp.minimum(s, n_main - 1) // n_blocks
        sblk = lambda s: lax.rem(jnp.minimum(s, n_main - 1), n_blocks)
        state_spec = pl.BlockSpec((1, SAMPLE_BLOCK, 1, RET_DK, RET_DV),
                                  lambda s: (layer, sblk(s), head(s), 0, 0))
        t_spec = pl.BlockSpec((1, RET_DK, DEC_BATCH), lambda s: (head(s), 0, 0))
        vg_spec = pl.BlockSpec((SAMPLE_BLOCK, RET_DV), lambda s: (sblk(s), head(s)))
        in_specs += [pl.BlockSpec((chunk, RET_VDIM), lambda s: (0, 0)), t_spec, t_spec, vg_spec, vg_spec,
                     pl.BlockSpec(memory_space=pl.ANY)]
        args += [y_meta, qt, kt, vs, gs, state_ret]
        if prev_out is not None:
            aliases = {len(args): 2}
            in_specs.append(pl.BlockSpec(memory_space=pl.ANY))
            args.append(prev_out)
        out_specs.append(state_spec)
        out_shape.append(jax.ShapeDtypeStruct(state_ret.shape, F32))
        scratch += [
            pltpu.VMEM((RET_HEADS, SMALL_ROWS, RET_DV), F32),
            pltpu.VMEM((STATE_RING, SAMPLE_BLOCK, RET_DK, RET_DV), F32),
            pltpu.SemaphoreType.DMA((STATE_RING,)),
        ]
    return pl.pallas_call(
        functools.partial(_ret_kernel, chunk=chunk, valid=valid, n_chunks=n_chunks, n_seq=n_seq,
                          fused=fused, aliased=bool(aliases), layer=layer if fused else None),
        grid=(n_steps,),
        in_specs=in_specs,
        out_specs=out_specs,
        out_shape=out_shape,
        scratch_shapes=scratch,
        input_output_aliases=aliases,
        compiler_params=_params("arbitrary"),
        name="retention",
    )(*args)


def _outproj_kernel(x_ref, y_ref, w_ref, o_ref):
    for n in range(D_MODEL // OUT_TILE):
        sl = slice(n * OUT_TILE, (n + 1) * OUT_TILE)
        o_ref[:, sl] = x_ref[:, sl] + jnp.dot(y_ref[...], w_ref[0, :, sl].astype(BF16),
                                              preferred_element_type=F32)


def _ret_outproj(x, y, w_out, layer):
    x_spec = pl.BlockSpec((OUTPROJ_ROW_TILE, D_MODEL), lambda i: (i, 0))
    return pl.pallas_call(
        _outproj_kernel,
        grid=(N_ROWS // OUTPROJ_ROW_TILE,),
        in_specs=[
            x_spec,
            pl.BlockSpec((OUTPROJ_ROW_TILE, RET_VDIM), lambda i: (i, 0)),
            pl.BlockSpec((1, RET_VDIM, D_MODEL), lambda i: (layer, 0, 0), pipeline_mode=pl.Buffered(1)),
        ],
        out_specs=x_spec,
        out_shape=jax.ShapeDtypeStruct((N_ROWS, D_MODEL), F32),
        compiler_params=_params("arbitrary"),
        name="ret_outproj",
    )(x, y, w_out)


def _rope_tables():
    pos_small = np.zeros((SMALL_ROWS,), np.int32)
    pos_small[:N_META] = np.arange(N_META)
    pos_small[N_META:N_META + DEC_BATCH] = PAST_LEN
    pos = jnp.asarray(np.concatenate([pos_small, N_META + np.arange(SEQ)]))
    half = RET_DK // 2
    inv = ROPE_BASE ** (-jnp.arange(half, dtype=F32) / half)
    ang = pos.astype(F32)[:, None] * inv[None, :]

    def per_row(t):
        return jnp.concatenate([t[:SMALL_ROWS]] + [t[SMALL_ROWS:]] * BATCH, axis=0)

    return per_row(jnp.cos(ang)), per_row(jnp.sin(ang))


def kernel(x_prompt, x_sample, state_pool, state_ret, meta_tokens, pool_norm, pool_w, pool_scale,
           ret_norm, ret_w_in, ret_w_out, mlp_norm, mlp_w_up, mlp_w_down, final_norm):
    assert x_prompt.shape == (BATCH, SEQ, D_MODEL) and x_sample.shape == (DEC_BATCH, 1, D_MODEL)
    pad_rows = SMALL_ROWS - N_META - DEC_BATCH
    x_small = jnp.concatenate([
        meta_tokens.astype(F32),
        x_sample.reshape(DEC_BATCH, D_MODEL),
        jnp.zeros((pad_rows, D_MODEL), F32),
    ], axis=0)
    x = None
    log_gamma = jnp.log1p(-jnp.exp2(-5.0 - jnp.arange(RET_HEADS, dtype=F32)))
    cos, sin = _rope_tables()
    state_slots = state_pool.transpose(0, 2, 1, 3)
    carry_sums = _pool_carry_sums(state_slots)

    pool_p, pool_s, ret_p = [], [], []
    ret_s = None
    for i in range(DEPTH):
        j = i // 2
        if i % 2 == 0:
            carry_ext = jnp.pad(carry_sums[j], ((N_META, pad_rows), (0, 0)))
            if x is None:
                xs, xr, block0 = x_small, x_prompt.reshape(BATCH * SEQ, D_MODEL), 0
            else:
                xs, xr, block0 = x, x, SMALL_ROWS // POOL_TILE
            x, u_small, u_tail = _pool_layer(xs, xr, block0, carry_ext, pool_norm[j], pool_scale[j], pool_w, j)
            pool_p.append(u_tail[:, 1:])
            pool_s.append(u_small)
        else:
            qkvg = _ret_proj(x, ret_norm[j], ret_w_in, j, cos, sin)
            zero_state = jnp.zeros((RET_HEADS, RET_DK, RET_DV), F32)
            y_meta, s_meta = _retention(log_gamma, qkvg, zero_state, None,
                                        n_seq=1, n_chunks=1, chunk=RET_CHUNK, valid=N_META)
            qkvg_s = qkvg[N_META:N_META + DEC_BATCH]
            qt = qkvg_s[:, :D_MODEL].reshape(DEC_BATCH, RET_HEADS, RET_DK).transpose(1, 2, 0)
            kt = qkvg_s[:, D_MODEL:2 * D_MODEL].reshape(DEC_BATCH, RET_HEADS, RET_DK).transpose(1, 2, 0)
            sample = (y_meta, qt, kt, qkvg_s[:, 2 * D_MODEL:2 * D_MODEL + RET_VDIM].astype(F32),
                      qkvg_s[:, 2 * D_MODEL + RET_VDIM:].astype(F32), state_ret, ret_s, j)
            y, s_real, ret_s = _retention(log_gamma, qkvg, s_meta[0], sample, n_seq=BATCH,
                                          n_chunks=SEQ // RET_CHUNK, chunk=RET_CHUNK, valid=RET_CHUNK)
            ret_p.append(s_real)
            x = _ret_outproj(x, y, ret_w_out, j)
        x = _mlp_layer(x, mlp_norm[i], mlp_w_up, mlp_w_down, i, final_norm, i == DEPTH - 1)

    y_prompt = x[0].reshape(BATCH, SEQ, D_MODEL)
    y_sample = x[1].reshape(DEC_BATCH, 1, D_MODEL)
    new_pool_sample = _pool_sample_states(state_slots, *pool_s).transpose(0, 2, 1, 3)
    return (y_prompt, y_sample, jnp.stack(pool_p, axis=0), new_pool_sample,
            jnp.stack(ret_p, axis=0), ret_s)
```

```python
import functools

import numpy as np
import jax
import jax.numpy as jnp
from jax import lax
from jax.experimental import pallas as pl
from jax.experimental.pallas import tpu as pltpu

D_MODEL = 2048
BATCH = 4
SEQ = 2048
DEPTH = 4
DEC_BATCH = 128
PAST_LEN = 16384
N_META = 16
POOL_WINDOWS = (2, 4, 8, 16)
POOL_GROUP_DIM = D_MODEL // len(POOL_WINDOWS)
POOL_BUF = max(POOL_WINDOWS) - 1
RET_HEADS = 8
RET_DK = D_MODEL // RET_HEADS
RET_DV = 2 * RET_DK
RET_VDIM = RET_HEADS * RET_DV
RET_IN = 2 * D_MODEL + 2 * RET_VDIM
RET_CHUNK = 128
ROPE_BASE = 10000.0
D_FF = 4 * D_MODEL
EPS = 1e-6

SMALL_ROWS = 256
N_ROWS = SMALL_ROWS + BATCH * SEQ
ROW_TILE = N_ROWS // 8
POOL_TILE = 256
POOL_TILES_PER_BATCH = SEQ // POOL_TILE
POOL_STATE_ROWS = 32
POOL_CARRY = 16
FF_TILE = 1024
FF_CHUNK = 512
PROJ_TILE = 2048
OUT_TILE = 512
OUTPROJ_ROW_TILE = N_ROWS // 16
SAMPLE_BLOCK = 8
CHUNK_SPLIT = 2
WRAP_REAL_ROWS = ROW_TILE - SMALL_ROWS
POOL_RING = 3
STATE_RING = 3
VMEM_CAPACITY = 64 * 1024 * 1024
SUBLANES, LANES = 8, 128
VMEM_LIMIT = 60 * 1024 * 1024

F32 = jnp.float32
BF16 = jnp.bfloat16


def _rmsnorm(x, g):
    return x * lax.rsqrt(jnp.mean(x * x, axis=-1, keepdims=True) + EPS) * g


def _params(*semantics):
    return pltpu.CompilerParams(dimension_semantics=semantics, vmem_limit_bytes=VMEM_LIMIT)


def _poolsum_kernel(s_ref, o_ref):
    for gi, w in enumerate(POOL_WINDOWS):
        sl = slice(gi * POOL_GROUP_DIM, (gi + 1) * POOL_GROUP_DIM)
        acc = s_ref[0, POOL_BUF - 1, :, sl]
        for r in range(POOL_BUF - (w - 1), POOL_BUF - 1):
            acc = acc + s_ref[0, r, :, sl]
        o_ref[0, :, sl] = acc


def _pool_carry_sums(state_slots):
    n_layers = state_slots.shape[0]
    n_rows = POOL_STATE_ROWS
    return pl.pallas_call(
        _poolsum_kernel,
        grid=(n_layers, DEC_BATCH // n_rows),
        in_specs=[pl.BlockSpec((1, POOL_BUF, n_rows, D_MODEL), lambda j, n: (j, 0, n, 0))],
        out_specs=pl.BlockSpec((1, n_rows, D_MODEL), lambda j, n: (j, n, 0)),
        out_shape=jax.ShapeDtypeStruct((n_layers, DEC_BATCH, D_MODEL), F32),
        compiler_params=_params("arbitrary", "arbitrary"),
        name="pool_carry_sums",
    )(state_slots)


def _pool_state_kernel(s_ref, u0_ref, u1_ref, o_ref):
    for r in range(POOL_BUF - 1):
        o_ref[0, r] = s_ref[0, r + 1]
    rows = pl.ds(pl.multiple_of(N_META + pl.program_id(1) * POOL_STATE_ROWS, SUBLANES), POOL_STATE_ROWS)

    @pl.when(pl.program_id(0) == 0)
    def _():
        o_ref[0, POOL_BUF - 1] = u0_ref[rows, :]

    @pl.when(pl.program_id(0) == 1)
    def _():
        o_ref[0, POOL_BUF - 1] = u1_ref[rows, :]


def _pool_sample_states(state_slots, u_small0, u_small1):
    assert state_slots.shape[0] == 2
    u_spec = pl.BlockSpec((SMALL_ROWS, D_MODEL), lambda j, n: (0, 0))
    state_spec = pl.BlockSpec((1, POOL_BUF, POOL_STATE_ROWS, D_MODEL), lambda j, n: (j, 0, n, 0))
    return pl.pallas_call(
        _pool_state_kernel,
        grid=(state_slots.shape[0], DEC_BATCH // POOL_STATE_ROWS),
        in_specs=[state_spec, u_spec, u_spec],
        out_specs=state_spec,
        out_shape=jax.ShapeDtypeStruct(state_slots.shape, F32),
        compiler_params=_params("arbitrary", "arbitrary"),
        name="pool_sample_states",
    )(state_slots, u_small0, u_small1)


def _pool_kernel(xs_hbm, xr_hbm, carry_ref, g_ref, sc_ref, w_ref, o_ref, usmall_ref, utail_ref,
                 prev_scr, meta_scr, w_scr, ring, ring_sems, *, real_block0):
    t = pl.program_id(0)
    is_small = t == 0
    chunk = lax.rem(jnp.maximum(t - 1, 0), POOL_TILES_PER_BATCH)

    def small_copy():
        return pltpu.make_async_copy(xs_hbm.at[pl.ds(0, POOL_TILE), :], ring.at[0], ring_sems.at[0])

    def real_copy(tile):
        slot = tile % POOL_RING
        row0 = (real_block0 + tile - 1) * POOL_TILE
        if not isinstance(tile, int):
            row0 = pl.multiple_of(row0, POOL_TILE)
        return pltpu.make_async_copy(xr_hbm.at[pl.ds(row0, POOL_TILE), :], ring.at[slot], ring_sems.at[slot])

    @pl.when(is_small)
    def _():
        small_copy().start()
        for tile in range(1, POOL_RING - 1):
            real_copy(tile).start()
        small_copy().wait()

    @pl.when(t + POOL_RING - 1 < pl.num_programs(0))
    def _():
        real_copy(t + POOL_RING - 1).start()

    @pl.when(t > 0)
    def _():
        real_copy(t).wait()

    x = ring[lax.rem(t, POOL_RING)]
    u = _rmsnorm(x, g_ref[...])

    @pl.when(is_small)
    def _():
        prev_scr[...] = jnp.zeros((POOL_CARRY, D_MODEL), F32)
        meta_scr[...] = u[N_META - POOL_CARRY:N_META, :]
        usmall_ref[...] = u
        w_scr[...] = w_ref[0].astype(BF16)

    @pl.when(jnp.logical_and(t > 0, chunk == 0))
    def _():
        prev_scr[...] = meta_scr[...]

    ext = jnp.concatenate([prev_scr[...], u], axis=0)
    row = lax.broadcasted_iota(jnp.int32, (POOL_TILE, 1), 0)
    meta_row = jnp.logical_and(is_small, row < N_META)
    sample_row = jnp.logical_and(is_small, row >= N_META)
    for gi, w in enumerate(POOL_WINDOWS):
        sl = slice(gi * POOL_GROUP_DIM, (gi + 1) * POOL_GROUP_DIM)
        s = ext[:, sl]
        shift = 1
        while shift < w:
            s = s + pltpu.roll(s, shift, 0)
            shift *= 2
        ug = u[:, sl]
        ws = jnp.where(sample_row, carry_ref[:, sl] + ug, s[POOL_CARRY:, :])
        inv_cnt = 1.0 / jnp.where(meta_row, jnp.minimum(w, row + 1), w).astype(F32)
        d = (ws * inv_cnt - ug).astype(BF16)
        y = jnp.dot(d, w_scr[gi], preferred_element_type=F32)
        o_ref[:, sl] = x[:, sl] + y * sc_ref[:, sl]

    tail = u[POOL_TILE - POOL_CARRY:, :]
    prev_scr[...] = tail

    @pl.when(jnp.logical_and(t > 0, chunk == POOL_TILES_PER_BATCH - 1))
    def _():
        utail_ref[0] = tail


def _pool_layer(x_small, x_real, real_block0, carry_ext, norm_g, scale, pool_w, layer):
    n_tiles = N_ROWS // POOL_TILE
    row_spec = pl.BlockSpec((POOL_TILE, D_MODEL), lambda t: (t, 0))
    first_spec = pl.BlockSpec((POOL_TILE, D_MODEL), lambda t: (0, 0))
    vec_spec = pl.BlockSpec((1, D_MODEL), lambda t: (0, 0))
    n_groups = len(POOL_WINDOWS)
    return pl.pallas_call(
        functools.partial(_pool_kernel, real_block0=real_block0),
        grid=(n_tiles,),
        in_specs=[
            pl.BlockSpec(memory_space=pl.ANY),
            pl.BlockSpec(memory_space=pl.ANY),
            first_spec,
            vec_spec,
            vec_spec,
            pl.BlockSpec((1, n_groups, POOL_GROUP_DIM, POOL_GROUP_DIM), lambda t: (layer, 0, 0, 0)),
        ],
        out_specs=[
            row_spec,
            first_spec,
            pl.BlockSpec((1, POOL_CARRY, D_MODEL),
                         lambda t: (jnp.maximum(t - 1, 0) // POOL_TILES_PER_BATCH, 0, 0)),
        ],
        out_shape=[
            jax.ShapeDtypeStruct((N_ROWS, D_MODEL), F32),
            jax.ShapeDtypeStruct((SMALL_ROWS, D_MODEL), F32),
            jax.ShapeDtypeStruct((BATCH, POOL_CARRY, D_MODEL), F32),
        ],
        scratch_shapes=[
            pltpu.VMEM((POOL_CARRY, D_MODEL), F32),
            pltpu.VMEM((POOL_CARRY, D_MODEL), F32),
            pltpu.VMEM((n_groups, POOL_GROUP_DIM, POOL_GROUP_DIM), BF16),
            pltpu.VMEM((POOL_RING, POOL_TILE, D_MODEL), F32),
            pltpu.SemaphoreType.DMA((POOL_RING,)),
        ],
        compiler_params=_params("arbitrary"),
        name="pool_layer",
    )(x_small, x_real, carry_ext, norm_g.reshape(1, D_MODEL), scale.reshape(1, D_MODEL), pool_w)


def _consume_row_tile(x_hbm, x_scr, x_sems, consume, wrap=False):
    i = pl.program_id(0)
    j = pl.program_id(1)
    last = pl.num_programs(0) - 1

    def whole_tile_copy(tile):
        row0 = SMALL_ROWS + tile * ROW_TILE if wrap else tile * ROW_TILE
        rows = pl.ds(pl.multiple_of(row0, SUBLANES), ROW_TILE)
        return [pltpu.make_async_copy(x_hbm.at[rows, :], x_scr, x_sems.at[0])]

    def wrapped_tile_copy():
        return [
            pltpu.make_async_copy(x_hbm.at[pl.ds(N_ROWS - WRAP_REAL_ROWS, WRAP_REAL_ROWS), :],
                                  x_scr.at[pl.ds(0, WRAP_REAL_ROWS), :], x_sems.at[0]),
            pltpu.make_async_copy(x_hbm.at[pl.ds(0, SMALL_ROWS), :],
                                  x_scr.at[pl.ds(WRAP_REAL_ROWS, SMALL_ROWS), :], x_sems.at[1]),
        ]

    def for_tile(tile, action):
        if not wrap:
            for c in whole_tile_copy(tile):
                action(c)
            return

        @pl.when(tile < last)
        def _():
            for c in whole_tile_copy(tile):
                action(c)

        @pl.when(tile == last)
        def _():
            for c in wrapped_tile_copy():
                action(c)

    @pl.when(jnp.logical_and(i == 0, j == 0))
    def _():
        for_tile(i, lambda c: c.start())

    @pl.when(j == 0)
    def _():
        for_tile(i, lambda c: c.wait())
        consume()

    @pl.when(jnp.logical_and(j == 1, i < last))
    def _():
        for_tile(i + 1, lambda c: c.start())


def _mlp_kernel(*refs, final):
    if final:
        x_hbm, g_ref, wu_ref, wd_ref, gf_ref, o_ref, ys_ref, x_scr, u_scr, x_sems = refs
    else:
        x_hbm, g_ref, wu_ref, wd_ref, gf_ref, o_ref, x_scr, u_scr, x_sems = refs
    j = pl.program_id(1)

    def start_tile():
        x = x_scr[...]
        u_scr[...] = _rmsnorm(x, g_ref[...]).astype(BF16)
        o_ref[...] = x

    _consume_row_tile(x_hbm, x_scr, x_sems, start_tile, wrap=final)

    for c in range(FF_TILE // FF_CHUNK):
        ff = slice(c * FF_CHUNK, (c + 1) * FF_CHUNK)
        h = jnp.dot(u_scr[...], wu_ref[0, :, ff].astype(BF16), preferred_element_type=F32)
        h = jnp.square(jnp.maximum(h, 0.0)).astype(BF16)
        for n in range(D_MODEL // OUT_TILE):
            sl = slice(n * OUT_TILE, (n + 1) * OUT_TILE)
            o_ref[:, sl] += jnp.dot(h, wd_ref[0, ff, sl].astype(BF16), preferred_element_type=F32)

    if final:
        @pl.when(j == pl.num_programs(1) - 1)
        def _():
            o_ref[...] = _rmsnorm(o_ref[...], gf_ref[...])

            @pl.when(pl.program_id(0) == pl.num_programs(0) - 1)
            def _():
                rows = pl.ds(WRAP_REAL_ROWS + N_META, DEC_BATCH)
                copy = pltpu.make_async_copy(o_ref.at[rows, :], ys_ref, x_sems.at[2])
                copy.start()
                copy.wait()


def _mlp_layer(x, norm_g, w_up, w_down, layer, final_g, final):
    row_spec = pl.BlockSpec((ROW_TILE, D_MODEL), lambda i, j: (i, 0))
    vec_spec = pl.BlockSpec((1, D_MODEL), lambda i, j: (0, 0))
    if final:
        out_specs = [row_spec, pl.BlockSpec(memory_space=pl.ANY)]
        out_shape = [jax.ShapeDtypeStruct((BATCH * SEQ, D_MODEL), F32),
                     jax.ShapeDtypeStruct((DEC_BATCH, D_MODEL), F32)]
    else:
        out_specs, out_shape = row_spec, jax.ShapeDtypeStruct((N_ROWS, D_MODEL), F32)
    return pl.pallas_call(
        functools.partial(_mlp_kernel, final=final),
        grid=(N_ROWS // ROW_TILE, D_FF // FF_TILE),
        in_specs=[
            pl.BlockSpec(memory_space=pl.ANY),
            vec_spec,
            pl.BlockSpec((1, D_MODEL, FF_TILE), lambda i, j: (layer, 0, j)),
            pl.BlockSpec((1, FF_TILE, D_MODEL), lambda i, j: (layer, j, 0)),
            vec_spec,
        ],
        out_specs=out_specs,
        out_shape=out_shape,
        scratch_shapes=[
            pltpu.VMEM((ROW_TILE, D_MODEL), F32),
            pltpu.VMEM((ROW_TILE, D_MODEL), BF16),
            pltpu.SemaphoreType.DMA((3,)),
        ],
        compiler_params=pltpu.CompilerParams(dimension_semantics=("arbitrary", "arbitrary"),
                                             vmem_limit_bytes=VMEM_CAPACITY),
        name="mlp_layer",
    )(x, norm_g.reshape(1, D_MODEL), w_up, w_down, final_g.reshape(1, D_MODEL))


def _proj_kernel(x_hbm, g_ref, w_ref, cos_ref, sin_ref, o_ref, x_scr, u_scr, x_sems):
    j = pl.program_id(1)
    n_rope_tiles = 2 * D_MODEL // PROJ_TILE

    def start_tile():
        u_scr[...] = _rmsnorm(x_scr[...], g_ref[...]).astype(BF16)

    _consume_row_tile(x_hbm, x_scr, x_sems, start_tile)

    is_rope = j < n_rope_tiles
    k_scale = jnp.where(j >= n_rope_tiles // 2, RET_DK ** -0.5, 1.0).astype(F32)
    cos = jnp.where(is_rope, cos_ref[...] * k_scale, 1.0)
    sin = jnp.where(is_rope, sin_ref[...] * k_scale, 0.0)
    half = RET_DK // 2
    for hh in range(PROJ_TILE // RET_DK):
        acc = jnp.dot(u_scr[...], w_ref[0, :, hh * RET_DK:(hh + 1) * RET_DK].astype(BF16),
                      preferred_element_type=F32)
        x1 = acc[:, :half]
        x2 = acc[:, half:]
        o_ref[:, hh * RET_DK:hh * RET_DK + half] = (x1 * cos - x2 * sin).astype(BF16)
        o_ref[:, hh * RET_DK + half:(hh + 1) * RET_DK] = (x1 * sin + x2 * cos).astype(BF16)


def _ret_proj(x, norm_g, w_in, layer, cos, sin):
    vec_spec = pl.BlockSpec((1, D_MODEL), lambda i, j: (0, 0))
    rope_spec = pl.BlockSpec((ROW_TILE, RET_DK // 2), lambda i, j: (i, 0))
    return pl.pallas_call(
        _proj_kernel,
        grid=(N_ROWS // ROW_TILE, RET_IN // PROJ_TILE),
        in_specs=[
            pl.BlockSpec(memory_space=pl.ANY),
            vec_spec,
            pl.BlockSpec((1, D_MODEL, PROJ_TILE), lambda i, j: (layer, 0, j)),
            rope_spec,
            rope_spec,
        ],
        out_specs=pl.BlockSpec((ROW_TILE, PROJ_TILE), lambda i, j: (i, j)),
        out_shape=jax.ShapeDtypeStruct((N_ROWS, RET_IN), BF16),
        scratch_shapes=[
            pltpu.VMEM((ROW_TILE, D_MODEL), F32),
            pltpu.VMEM((ROW_TILE, D_MODEL), BF16),
            pltpu.SemaphoreType.DMA((1,)),
        ],
        compiler_params=pltpu.CompilerParams(dimension_semantics=("arbitrary", "arbitrary"),
                                             vmem_limit_bytes=VMEM_CAPACITY),
        name="ret_proj",
    )(x, norm_g.reshape(1, D_MODEL), w_in, cos, sin)


def _head_norm_gate(o, g):
    mu = jnp.mean(o, axis=-1, keepdims=True)
    oc = o - mu
    var = jnp.mean(oc * oc, axis=-1, keepdims=True)
    return oc * lax.rsqrt(var + EPS) * (g * jax.nn.sigmoid(g))


def _chunk_heads(heads, lg_ref, q_ref, k_ref, v_ref, g_ref, s_scr, y_ref, chunk, valid):
    row = lax.broadcasted_iota(jnp.int32, (chunk, 1), 0).astype(F32)
    col = lax.broadcasted_iota(jnp.int32, (1, chunk), 1).astype(F32)
    diff = row - col
    for h in heads:
        lg = lg_ref[h]
        qh = q_ref[:, h * RET_DK:(h + 1) * RET_DK]
        kh = k_ref[:, h * RET_DK:(h + 1) * RET_DK]
        vh = v_ref[:, h * RET_DV:(h + 1) * RET_DV]
        decay = jnp.where(diff >= 0, jnp.exp(jnp.maximum(diff, 0.0) * lg), 0.0)
        scores = lax.dot_general(qh, kh, (((1,), (1,)), ((), ())), preferred_element_type=F32) * decay
        inner = jnp.dot(scores.astype(BF16), vh, preferred_element_type=F32)
        st = s_scr[h]
        cross = jnp.dot(qh, st.astype(BF16), preferred_element_type=F32) * jnp.exp((row + 1.0) * lg)
        o = inner + cross
        kd = kh.astype(F32) * jnp.exp((valid - 1.0 - row) * lg)
        if valid < chunk:
            kd = jnp.where(row < valid, kd, 0.0)
        kd = kd.astype(BF16)
        s_dec = jnp.exp(jnp.full((1, RET_DV), valid, F32) * lg)
        s_scr[h] = st * s_dec + lax.dot_general(kd, vh, (((0,), (0,)), ((), ())),
                                               preferred_element_type=F32)
        gh = g_ref[:, h * RET_DV:(h + 1) * RET_DV].astype(F32)
        y_ref[:, h * RET_DV:(h + 1) * RET_DV] = _head_norm_gate(o, gh).astype(BF16)


def _sample_head(h, nb, lg_ref, qt_ref, kt_ref, v_ref, g_ref, s_ref, so_ref, ysm_scr):
    lg = lg_ref[h]
    gamma = jnp.exp(jnp.full((1, LANES), 1.0, F32) * lg)
    sample = lax.broadcasted_iota(jnp.int32, (DEC_BATCH, LANES), 0)
    row8 = lax.broadcasted_iota(jnp.int32, (SAMPLE_BLOCK, 1), 0)
    qt = qt_ref[0]
    kt = kt_ref[0]
    o_all = jnp.zeros((SAMPLE_BLOCK, RET_DV), F32)
    for n in range(SAMPLE_BLOCK):
        onehot = (sample == nb * SAMPLE_BLOCK + n).astype(BF16)
        qb = jnp.dot(qt, onehot, preferred_element_type=F32)
        kb = jnp.dot(kt, onehot, preferred_element_type=F32)
        v_row = v_ref[n:n + 1, :]
        parts = []
        for vt in range(RET_DV // LANES):
            sl = slice(vt * LANES, (vt + 1) * LANES)
            s_new = s_ref[n, :, sl] * gamma + kb * v_row[:, sl]
            so_ref[0, n, 0, :, sl] = s_new
            parts.append(jnp.sum(s_new * qb, axis=0, keepdims=True))
        o_n = jnp.concatenate(parts, axis=1)
        o_all = jnp.where(row8 == n, o_n, o_all)
    rows = pl.ds(pl.multiple_of(N_META + nb * SAMPLE_BLOCK, SAMPLE_BLOCK), SAMPLE_BLOCK)
    ysm_scr[h, rows, :] = _head_norm_gate(o_all, g_ref[...])


def _ret_kernel(*refs, chunk, valid, n_chunks, n_seq, fused, aliased, layer):
    if not fused:
        lg_ref, q_ref, k_ref, v_ref, g_ref, s0_ref, y_ref, sout_ref, s_scr = refs
        s_scr[...] = s0_ref[...]
        _chunk_heads(range(RET_HEADS), lg_ref, q_ref, k_ref, v_ref, g_ref, s_scr, y_ref, chunk, valid)
        sout_ref[0] = s_scr[...]
        return

    (lg_ref, q_ref, k_ref, v_ref, g_ref, s0_ref, ymeta_ref, qt_ref, kt_ref, vs_ref, gs_ref,
     st_hbm) = refs[:12]
    y_ref, sout_ref, sto_ref, s_scr, ysm_scr, ring, ring_sems = refs[12 + (1 if aliased else 0):]
    s = pl.program_id(0)
    n_main = n_seq * n_chunks * CHUNK_SPLIT
    heads_per_step = RET_HEADS // CHUNK_SPLIT
    n_blocks = DEC_BATCH // SAMPLE_BLOCK

    def state_copy(step):
        slot = step % STATE_RING
        row0 = (step % n_blocks) * SAMPLE_BLOCK
        if not isinstance(step, int):
            row0 = pl.multiple_of(row0, SAMPLE_BLOCK)
        return pltpu.make_async_copy(st_hbm.at[layer, pl.ds(row0, SAMPLE_BLOCK), step // n_blocks],
                                     ring.at[slot], ring_sems.at[slot])

    @pl.when(s < n_main)
    def _():
        part = lax.rem(s, CHUNK_SPLIT)
        c = lax.rem(s // CHUNK_SPLIT, n_chunks)

        @pl.when(s == 0)
        def _():
            for step in range(STATE_RING - 1):
                state_copy(step).start()
            ysm_scr[...] = jnp.zeros(ysm_scr.shape, F32)
            for h in range(RET_HEADS):
                ysm_scr[h, 0:N_META, :] = ymeta_ref[0:N_META, h * RET_DV:(h + 1) * RET_DV].astype(F32)

        @pl.when(s + STATE_RING - 1 < n_main)
        def _():
            state_copy(s + STATE_RING - 1).start()

        @pl.when(jnp.logical_and(c == 0, part == 0))
        def _():
            s_scr[...] = s0_ref[...]

        for p in range(CHUNK_SPLIT):
            @pl.when(part == p)
            def _(p=p):
                _chunk_heads(range(p * heads_per_step, (p + 1) * heads_per_step),
                             lg_ref, q_ref, k_ref, v_ref, g_ref, s_scr, y_ref, chunk, valid)

        @pl.when(jnp.logical_and(c == n_chunks - 1, part == CHUNK_SPLIT - 1))
        def _():
            sout_ref[0] = s_scr[...]

        state_copy(s).wait()
        _sample_head(s // n_blocks, lax.rem(s, n_blocks), lg_ref, qt_ref, kt_ref, vs_ref, gs_ref,
                     ring.at[lax.rem(s, STATE_RING)], sto_ref, ysm_scr)

    @pl.when(s >= n_main)
    def _():
        rows = pl.ds(pl.multiple_of((s - n_main) * chunk, chunk), chunk)
        for h in range(RET_HEADS):
            y_ref[:, h * RET_DV:(h + 1) * RET_DV] = ysm_scr[h, rows, :].astype(BF16)


def _retention(log_gamma, qkvg, s0, sample, *, n_seq, n_chunks, chunk, valid):
    k_col = 1
    v_col = 2 * D_MODEL // RET_VDIM
    fused = sample is not None
    if not fused:
        n_steps, n_rows = n_seq * n_chunks, n_seq * n_chunks * chunk
        blk = lambda s: s
        y_blk = lambda s: s
        seq = lambda s: s // n_chunks
    else:
        y_meta, qt, kt, vs, gs, state_ret, prev_out, layer = sample
        n_small = SMALL_ROWS // chunk
        n_main = n_seq * n_chunks * CHUNK_SPLIT
        assert n_main == RET_HEADS * (DEC_BATCH // SAMPLE_BLOCK)
        n_steps, n_rows = n_main + n_small, N_ROWS
        blk = lambda s: n_small + jnp.minimum(s, n_main - 1) // CHUNK_SPLIT
        y_blk = lambda s: jnp.where(s < n_main, n_small + s // CHUNK_SPLIT, s - n_main)
        seq = lambda s: jnp.minimum(s, n_main - 1) // (CHUNK_SPLIT * n_chunks)
    in_specs = [
        pl.BlockSpec(memory_space=pltpu.SMEM),
        pl.BlockSpec((chunk, D_MODEL), lambda s: (blk(s), 0)),
        pl.BlockSpec((chunk, D_MODEL), lambda s: (blk(s), k_col)),
        pl.BlockSpec((chunk, RET_VDIM), lambda s: (blk(s), v_col)),
        pl.BlockSpec((chunk, RET_VDIM), lambda s: (blk(s), v_col + 1)),
        pl.BlockSpec((RET_HEADS, RET_DK, RET_DV), lambda s: (0, 0, 0)),
    ]
    args = [log_gamma, qkvg, qkvg, qkvg, qkvg, s0]
    out_specs = [
        pl.BlockSpec((chunk, RET_VDIM), lambda s: (y_blk(s), 0)),
        pl.BlockSpec((1, RET_HEADS, RET_DK, RET_DV), lambda s: (seq(s), 0, 0, 0)),
    ]
    out_shape = [
        jax.ShapeDtypeStruct((n_rows, RET_VDIM), BF16),
        jax.ShapeDtypeStruct((n_seq, RET_HEADS, RET_DK, RET_DV), F32),
    ]
    scratch = [pltpu.VMEM((RET_HEADS, RET_DK, RET_DV), F32)]
    aliases = {}
    if fused:
        n_blocks = DEC_BATCH // SAMPLE_BLOCK
        head = lambda s: jnp.minimum(s, n_main - 1) // n_blocks
        sblk = lambda s: lax.rem(jnp.minimum(s, n_main - 1), n_blocks)
        state_spec = pl.BlockSpec((1, SAMPLE_BLOCK, 1, RET_DK, RET_DV),
                                  lambda s: (layer, sblk(s), head(s), 0, 0))
        t_spec = pl.BlockSpec((1, RET_DK, DEC_BATCH), lambda s: (head(s), 0, 0))
        vg_spec = pl.BlockSpec((SAMPLE_BLOCK, RET_DV), lambda s: (sblk(s), head(s)))
        in_specs += [pl.BlockSpec((chunk, RET_VDIM), lambda s: (0, 0)), t_spec, t_spec, vg_spec, vg_spec,
                     pl.BlockSpec(memory_space=pl.ANY)]
        args += [y_meta, qt, kt, vs, gs, state_ret]
        if prev_out is not None:
            aliases = {len(args): 2}
            in_specs.append(pl.BlockSpec(memory_space=pl.ANY))
            args.append(prev_out)
        out_specs.append(state_spec)
        out_shape.append(jax.ShapeDtypeStruct(state_ret.shape, F32))
        scratch += [
            pltpu.VMEM((RET_HEADS, SMALL_ROWS, RET_DV), F32),
            pltpu.VMEM((STATE_RING, SAMPLE_BLOCK, RET_DK, RET_DV), F32),
            pltpu.SemaphoreType.DMA((STATE_RING,)),
        ]
    return pl.pallas_call(
        functools.partial(_ret_kernel, chunk=chunk, valid=valid, n_chunks=n_chunks, n_seq=n_seq,
                          fused=fused, aliased=bool(aliases), layer=layer if fused else None),
        grid=(n_steps,),
        in_specs=in_specs,
        out_specs=out_specs,
        out_shape=out_shape,
        scratch_shapes=scratch,
        input_output_aliases=aliases,
        compiler_params=_params("arbitrary"),
        name="retention",
    )(*args)


def _outproj_kernel(x_ref, y_ref, w_ref, o_ref):
    for n in range(D_MODEL // OUT_TILE):
        sl = slice(n * OUT_TILE, (n + 1) * OUT_TILE)
        o_ref[:, sl] = x_ref[:, sl] + jnp.dot(y_ref[...], w_ref[0, :, sl].astype(BF16),
                                              preferred_element_type=F32)


def _ret_outproj(x, y, w_out, layer):
    x_spec = pl.BlockSpec((OUTPROJ_ROW_TILE, D_MODEL), lambda i: (i, 0))
    return pl.pallas_call(
        _outproj_kernel,
        grid=(N_ROWS // OUTPROJ_ROW_TILE,),
        in_specs=[
            x_spec,
            pl.BlockSpec((OUTPROJ_ROW_TILE, RET_VDIM), lambda i: (i, 0)),
            pl.BlockSpec((1, RET_VDIM, D_MODEL), lambda i: (layer, 0, 0), pipeline_mode=pl.Buffered(1)),
        ],
        out_specs=x_spec,
        out_shape=jax.ShapeDtypeStruct((N_ROWS, D_MODEL), F32),
        compiler_params=_params("arbitrary"),
        name="ret_outproj",
    )(x, y, w_out)


def _rope_tables():
    pos_small = np.zeros((SMALL_ROWS,), np.int32)
    pos_small[:N_META] = np.arange(N_META)
    pos_small[N_META:N_META + DEC_BATCH] = PAST_LEN
    pos = jnp.asarray(np.concatenate([pos_small, N_META + np.arange(SEQ)]))
    half = RET_DK // 2
    inv = ROPE_BASE ** (-jnp.arange(half, dtype=F32) / half)
    ang = pos.astype(F32)[:, None] * inv[None, :]

    def per_row(t):
        return jnp.concatenate([t[:SMALL_ROWS]] + [t[SMALL_ROWS:]] * BATCH, axis=0)

    return per_row(jnp.cos(ang)), per_row(jnp.sin(ang))


def kernel(x_prompt, x_sample, state_pool, state_ret, meta_tokens, pool_norm, pool_w, pool_scale,
           ret_norm, ret_w_in, ret_w_out, mlp_norm, mlp_w_up, mlp_w_down, final_norm):
    assert x_prompt.shape == (BATCH, SEQ, D_MODEL) and x_sample.shape == (DEC_BATCH, 1, D_MODEL)
    pad_rows = SMALL_ROWS - N_META - DEC_BATCH
    x_small = jnp.concatenate([
        meta_tokens.astype(F32),
        x_sample.reshape(DEC_BATCH, D_MODEL),
        jnp.zeros((pad_rows, D_MODEL), F32),
    ], axis=0)
    x = None
    log_gamma = jnp.log1p(-jnp.exp2(-5.0 - jnp.arange(RET_HEADS, dtype=F32)))
    cos, sin = _rope_tables()
    state_slots = state_pool.transpose(0, 2, 1, 3)
    carry_sums = _pool_carry_sums(state_slots)

    pool_p, pool_s, ret_p = [], [], []
    ret_s = None
    for i in range(DEPTH):
        j = i // 2
        if i % 2 == 0:
            carry_ext = jnp.pad(carry_sums[j], ((N_META, pad_rows), (0, 0)))
            if x is None:
                xs, xr, block0 = x_small, x_prompt.reshape(BATCH * SEQ, D_MODEL), 0
            else:
                xs, xr, block0 = x, x, SMALL_ROWS // POOL_TILE
            x, u_small, u_tail = _pool_layer(xs, xr, block0, carry_ext, pool_norm[j], pool_scale[j], pool_w, j)
            pool_p.append(u_tail[:, POOL_CARRY - POOL_BUF:])
            pool_s.append(u_small)
        else:
            qkvg = _ret_proj(x, ret_norm[j], ret_w_in, j, cos, sin)
            zero_state = jnp.zeros((RET_HEADS, RET_DK, RET_DV), F32)
            y_meta, s_meta = _retention(log_gamma, qkvg, zero_state, None,
                                        n_seq=1, n_chunks=1, chunk=RET_CHUNK, valid=N_META)
            qkvg_s = qkvg[N_META:N_META + DEC_BATCH]
            qt = qkvg_s[:, :D_MODEL].reshape(DEC_BATCH, RET_HEADS, RET_DK).transpose(1, 2, 0)
            kt = qkvg_s[:, D_MODEL:2 * D_MODEL].reshape(DEC_BATCH, RET_HEADS, RET_DK).transpose(1, 2, 0)
            sample = (y_meta, qt, kt, qkvg_s[:, 2 * D_MODEL:2 * D_MODEL + RET_VDIM].astype(F32),
                      qkvg_s[:, 2 * D_MODEL + RET_VDIM:].astype(F32), state_ret, ret_s, j)
            y, s_real, ret_s = _retention(log_gamma, qkvg, s_meta[0], sample, n_seq=BATCH,
                                          n_chunks=SEQ // RET_CHUNK, chunk=RET_CHUNK, valid=RET_CHUNK)
            ret_p.append(s_real)
            x = _ret_outproj(x, y, ret_w_out, j)
        x = _mlp_layer(x, mlp_norm[i], mlp_w_up, mlp_w_down, i, final_norm, i == DEPTH - 1)

    y_prompt = x[0].reshape(BATCH, SEQ, D_MODEL)
    y_sample = x[1].reshape(DEC_BATCH, 1, D_MODEL)
    new_pool_sample = _pool_sample_states(state_slots, *pool_s).transpose(0, 2, 1, 3)
    return (y_prompt, y_sample, jnp.stack(pool_p, axis=0), new_pool_sample,
            jnp.stack(ret_p, axis=0), ret_s)
```
